```python
import jax, jax.numpy as jnp
from jax import lax
import numpy as np

D_MODEL = 1024
BATCH = 2
SEQ = 8192
DEPTH = 4
DEC_BATCH = 4
DEC_SEQ = 4096
PAST_LEN = 128

N_MEM = 256
D_CONV = D_MODEL
CONV_K = 31
GLA_HEADS = 4
D_K = D_MODEL // 2
D_V = D_MODEL
DK_HEAD = D_K // GLA_HEADS
DV_HEAD = D_V // GLA_HEADS
GATE_RANK = 16
GATE_TEMP = 16.0
CHUNK = 64
XA_HEADS = 4
D_XA = D_MODEL
XA_HEAD = D_XA // XA_HEADS
N_BRANCH = 3
N_EXPERTS = 16
D_FF_EXPERT = 2048
CAPACITY_FACTOR = 2
EPS = 1e-6

SPLIT_SIZES = (2 * D_CONV, D_K, D_K, D_V, D_V, 2 * GATE_RANK, D_XA, N_BRANCH * D_MODEL)
P_IN = 2 * D_CONV + 2 * D_K + 2 * D_V + 2 * GATE_RANK + D_XA + N_BRANCH * D_MODEL

kernel_name = "hybrid_conv_gla_memxattn_ec_moe_encoder"


def rms_norm(x, g):
    xf = x.astype(jnp.float32)
    y = xf * lax.rsqrt(jnp.mean(xf * xf, axis=-1, keepdims=True) + EPS)
    return (y * g.astype(jnp.float32)).astype(x.dtype)


def split_cols(u):
    parts, off = [], 0
    for size in SPLIT_SIZES:
        parts.append(u[..., off:off + size])
        off += size
    return parts


def conv_branch(u, dw_w, dw_b, ln_g, ln_b, pw_w):
    a, b = jnp.split(u, 2, axis=-1)
    h = a * jax.nn.sigmoid(b)
    h = lax.conv_general_dilated(
        h, dw_w[:, None, :], window_strides=(1,),
        padding=[(CONV_K // 2, CONV_K // 2)],
        dimension_numbers=("NWC", "WIO", "NWC"),
        feature_group_count=D_CONV) + dw_b
    hf = h.astype(jnp.float32)
    mu = jnp.mean(hf, axis=-1, keepdims=True)
    var = jnp.mean(jnp.square(hf - mu), axis=-1, keepdims=True)
    hf = (hf - mu) * lax.rsqrt(var + EPS) * ln_g.astype(jnp.float32) + ln_b.astype(jnp.float32)
    h = jax.nn.silu(hf).astype(u.dtype)
    return h @ pw_w


def gla_chunked(q, k, v, log_a, inclusive):
    Bn, H, S, dk = q.shape
    dv = v.shape[-1]
    n = S // CHUNK

    def to_chunks(t):
        return t.reshape(Bn, H, n, CHUNK, t.shape[-1]).transpose(2, 0, 1, 3, 4)

    qc, kc, vc, ac = to_chunks(q), to_chunks(k), to_chunks(v), to_chunks(log_a)
    pos = jnp.arange(CHUNK)
    mask = (pos[:, None] >= pos[None, :]) if inclusive else (pos[:, None] > pos[None, :])

    def step(state, inp):
        qi, ki, vi, ai = inp
        b = jnp.cumsum(ai, axis=2)
        o_inter = jnp.einsum('bhcd,bhde->bhce', qi * jnp.exp(b), state)
        diff = b[:, :, :, None, :] - b[:, :, None, :, :]
        decay = jnp.exp(jnp.where(mask[:, :, None], diff, -jnp.inf))
        attn = jnp.einsum('bhid,bhjd,bhijd->bhij', qi, ki, decay)
        o_intra = jnp.einsum('bhij,bhje->bhie', attn, vi)
        b_last = b[:, :, -1, :]
        new_state = state * jnp.exp(b_last)[..., None] + jnp.einsum(
            'bhjd,bhje->bhde', ki * jnp.exp(b_last[:, :, None, :] - b), vi)
        return new_state, o_inter + o_intra

    state0 = jnp.zeros((Bn, H, dk, dv), jnp.float32)
    _, out = lax.scan(step, state0, (qc, kc, vc, ac))
    return out.transpose(1, 2, 0, 3, 4).reshape(Bn, H, S, dv)


def gla_branch(q, k, v, g_out, lr, w2, bg, norm_g, o_w):
    Bn, S, _ = q.shape

    def heads(t, d):
        return t.reshape(Bn, S, GLA_HEADS, d).transpose(0, 2, 1, 3)

    qh = heads(q, DK_HEAD) * (DK_HEAD ** -0.5)
    kh = heads(k, DK_HEAD)
    vh = heads(v, DV_HEAD)
    lr_f, lr_b = jnp.split(lr, 2, axis=-1)

    def log_decay(lr_d, w2_d, b_d):
        z = (lr_d @ w2_d + b_d).astype(jnp.float32)
        return heads(jax.nn.log_sigmoid(z) / GATE_TEMP, DK_HEAD)

    la_f = log_decay(lr_f, w2[0], bg[0])
    la_b = log_decay(lr_b, w2[1], bg[1])

    def flip(t):
        return jnp.flip(t, axis=2)

    o_f = gla_chunked(qh, kh, vh, la_f, True)
    o_b = flip(gla_chunked(flip(qh), flip(kh), flip(vh), flip(la_b), False))
    o = o_f + o_b
    o = o * lax.rsqrt(jnp.mean(o * o, axis=-1, keepdims=True) + EPS)
    o = o.transpose(0, 2, 1, 3).reshape(Bn, S, D_V) * norm_g.astype(jnp.float32)
    o = o.astype(q.dtype) * jax.nn.silu(g_out)
    return o @ o_w


def mem_attention(q, mem_n, w_kv, o_w):
    Bn, S, _ = q.shape
    M = mem_n.shape[1]
    kk, vv = jnp.split(mem_n @ w_kv, 2, axis=-1)
    qh = q.reshape(Bn, S, XA_HEADS, XA_HEAD)
    kh = kk.reshape(Bn, M, XA_HEADS, XA_HEAD)
    vh = vv.reshape(Bn, M, XA_HEADS, XA_HEAD)
    s = jnp.einsum('bshd,bmhd->bhsm', qh, kh).astype(jnp.float32) * (XA_HEAD ** -0.5)
    p = jax.nn.softmax(s, axis=-1).astype(vh.dtype)
    o = jnp.einsum('bhsm,bmhd->bshd', p, vh).reshape(Bn, S, D_XA)
    return o @ o_w


def expert_choice_ffn(h, w_router, w_gate, w_up, w_down):
    Bn, S, D = h.shape
    n_tok = Bn * S
    cap = CAPACITY_FACTOR * n_tok // N_EXPERTS
    xt = h.reshape(n_tok, D)
    aff = jax.nn.softmax((xt @ w_router).astype(jnp.float32), axis=-1)
    gate, idx = lax.top_k(aff.T, cap)
    xe = xt[idx]
    hid = jax.nn.silu(jnp.einsum('ecd,edf->ecf', xe, w_gate)) * jnp.einsum('ecd,edf->ecf', xe, w_up)
    ye = jnp.einsum('ecf,efd->ecd', hid, w_down) * gate[..., None].astype(xt.dtype)
    y = jnp.zeros_like(xt).at[idx.reshape(-1)].add(ye.reshape(-1, D))
    return y.reshape(Bn, S, D)


def run_trunk(x, mem, norm_mix_g, w_in, b_merge, conv_dw_w, conv_dw_b, conv_ln_g, conv_ln_b,
              conv_pw_w, gla_gate_w2, gla_gate_b, gla_norm_g, gla_o, mem_norm_g, w_mem_kv, xa_o,
              w_out, norm_ffn_g, w_router, w_gate_e, w_up_e, w_down_e, final_norm_g):
    Bn, S, D = x.shape
    for l in range(DEPTH):
        hn = rms_norm(x, norm_mix_g[l])
        conv_u, q, k, v, g_out, lr, xq, mg = split_cols(hn @ w_in[l])
        a = conv_branch(conv_u, conv_dw_w[l], conv_dw_b[l], conv_ln_g[l], conv_ln_b[l], conv_pw_w[l])
        b = gla_branch(q, k, v, g_out, lr, gla_gate_w2[l], gla_gate_b[l], gla_norm_g[l], gla_o[l])
        c = mem_attention(xq, rms_norm(mem, mem_norm_g[l]), w_mem_kv[l], xa_o[l])
        gates = jax.nn.sigmoid(mg + b_merge[l]).reshape(Bn, S, N_BRANCH, D)
        merged = gates[:, :, 0, :] * a + gates[:, :, 1, :] * b + gates[:, :, 2, :] * c
        x = x + merged @ w_out[l]
        x = x + expert_choice_ffn(rms_norm(x, norm_ffn_g[l]), w_router[l], w_gate_e[l], w_up_e[l], w_down_e[l])
    return rms_norm(x, final_norm_g)


def setup_inputs(seed: int = 0) -> dict:
    key = jax.random.key(seed)
    ks = jax.random.split(key, 32)
    f32 = jnp.float32
    L, D = DEPTH, D_MODEL

    def nrm(k, shape, scale):
        return jax.random.normal(k, shape, f32) * scale

    return {
        "x_prompt": nrm(ks[0], (BATCH, SEQ, D), 1.0),
        "x_sample": nrm(ks[1], (DEC_BATCH, DEC_SEQ, D), 1.0),
        "mem_prompt": nrm(ks[2], (BATCH, N_MEM, D), 1.0),
        "mem_sample": nrm(ks[3], (DEC_BATCH, N_MEM, D), 1.0),
        "norm_mix_g": 1.0 + nrm(ks[4], (L, D), 0.02),
        "w_in": nrm(ks[5], (L, D, P_IN), D ** -0.5),
        "b_merge": nrm(ks[6], (L, N_BRANCH * D), 0.02),
        "conv_dw_w": nrm(ks[7], (L, CONV_K, D_CONV), CONV_K ** -0.5),
        "conv_dw_b": nrm(ks[8], (L, D_CONV), 0.02),
        "conv_ln_g": 1.0 + nrm(ks[9], (L, D_CONV), 0.02),
        "conv_ln_b": nrm(ks[10], (L, D_CONV), 0.02),
        "conv_pw_w": nrm(ks[11], (L, D_CONV, D), D_CONV ** -0.5),
        "gla_gate_w2": nrm(ks[12], (L, 2, GATE_RANK, D_K), GATE_RANK ** -0.5),
        "gla_gate_b": nrm(ks[13], (L, 2, D_K), 0.1),
        "gla_norm_g": 1.0 + nrm(ks[14], (L, D_V), 0.02),
        "gla_o": nrm(ks[15], (L, D_V, D), D_V ** -0.5),
        "mem_norm_g": 1.0 + nrm(ks[16], (L, D), 0.02),
        "w_mem_kv": nrm(ks[17], (L, D, 2 * D_XA), D ** -0.5),
        "xa_o": nrm(ks[18], (L, D_XA, D), D_XA ** -0.5),
        "w_out": nrm(ks[19], (L, D, D), D ** -0.5),
        "norm_ffn_g": 1.0 + nrm(ks[20], (L, D), 0.02),
        "w_router": nrm(ks[21], (L, D, N_EXPERTS), D ** -0.5),
        "w_gate_e": nrm(ks[22], (L, N_EXPERTS, D, D_FF_EXPERT), D ** -0.5),
        "w_up_e": nrm(ks[23], (L, N_EXPERTS, D, D_FF_EXPERT), D ** -0.5),
        "w_down_e": nrm(ks[24], (L, N_EXPERTS, D_FF_EXPERT, D), D_FF_EXPERT ** -0.5),
        "final_norm_g": 1.0 + nrm(ks[25], (D,), 0.02),
    }


def reference(x_prompt, x_sample, mem_prompt, mem_sample, norm_mix_g, w_in, b_merge, conv_dw_w,
              conv_dw_b, conv_ln_g, conv_ln_b, conv_pw_w, gla_gate_w2, gla_gate_b, gla_norm_g, gla_o,
              mem_norm_g, w_mem_kv, xa_o, w_out, norm_ffn_g, w_router, w_gate_e, w_up_e, w_down_e,
              final_norm_g):
    y_prompt = run_trunk(x_prompt, mem_prompt, norm_mix_g, w_in, b_merge, conv_dw_w, conv_dw_b,
                         conv_ln_g, conv_ln_b, conv_pw_w, gla_gate_w2, gla_gate_b, gla_norm_g, gla_o,
                         mem_norm_g, w_mem_kv, xa_o, w_out, norm_ffn_g, w_router, w_gate_e, w_up_e,
                         w_down_e, final_norm_g)
    y_sample = run_trunk(x_sample, mem_sample, norm_mix_g, w_in, b_merge, conv_dw_w, conv_dw_b,
                         conv_ln_g, conv_ln_b, conv_pw_w, gla_gate_w2, gla_gate_b, gla_norm_g, gla_o,
                         mem_norm_g, w_mem_kv, xa_o, w_out, norm_ffn_g, w_router, w_gate_e, w_up_e,
                         w_down_e, final_norm_g)
    return (y_prompt, y_sample)
```

```python
import functools

import jax
import jax.numpy as jnp
from jax import lax
from jax.experimental import pallas as pl
from jax.experimental.pallas import tpu as pltpu

F32 = jnp.float32
BF16 = jnp.bfloat16
I32 = jnp.int32

D_MODEL = 1024
CONV_K = 31
CONV_HALO = 16
GLA_HEADS = 4
D_K = 512
D_V = 1024
DK_HEAD = D_K // GLA_HEADS
DV_HEAD = D_V // GLA_HEADS
GATE_RANK = 16
GATE_TEMP = 16.0
XA_HEADS = 4
XA_HEAD = D_MODEL // XA_HEADS
N_EXPERTS = 16
D_FF = 2048
CAPACITY_FACTOR = 2
EPS = 1e-6

LANE = 128
SUBLANE = 8
BF16_ROWS = 16
VMEM_LIMIT = 56 * 1024 * 1024

U_A, U_B, U_Q, U_K, U_V, U_G, U_XQ, U_MG = 0, 1024, 2048, 2560, 3072, 4096, 5120, 6144
U_COLS = 9216
LR_COLS = LANE

GLA_CHUNK = 128
COMBINE_WIN = LANE + BF16_ROWS


def _cparams(sem):
    return pltpu.CompilerParams(dimension_semantics=sem, vmem_limit_bytes=VMEM_LIMIT)


def _split_bf16(a):
    hi = a.astype(BF16)
    lo = (a - hi.astype(F32)).astype(BF16)
    return hi, lo


def _dot(a, b, dims=None):
    if dims is None:
        return jnp.dot(a, b, preferred_element_type=F32)
    return lax.dot_general(a, b, (dims, ((), ())), preferred_element_type=F32)


def _dot3(a, b, dims=None):
    ah, al = _split_bf16(a)
    bh, bl = _split_bf16(b)
    return _dot(ah, bh, dims) + (_dot(al, bh, dims) + _dot(ah, bl, dims))


def _rms(x, g):
    return x * lax.rsqrt(jnp.mean(x * x, axis=-1, keepdims=True) + EPS) * g


def _normmm_kernel(x_ref, g_ref, w_ref, *rest, with_lr):
    if with_lr:
        wlr_ref, u_ref, lr_ref, hn_ref = rest
    else:
        u_ref, hn_ref = rest

    @pl.when(pl.program_id(1) == 0)
    def _():
        hn = _rms(x_ref[...], g_ref[...]).astype(BF16)
        hn_ref[...] = hn
        if with_lr:
            lr_ref[...] = _dot(hn, wlr_ref[...])

    u_ref[...] = _dot(hn_ref[...], w_ref[...]).astype(u_ref.dtype)


def _norm_matmul(x, g, w, w_lr=None, *, tm, tn, name):
    n, d = x.shape
    cols = w.shape[1]
    with_lr = w_lr is not None
    in_specs = [pl.BlockSpec((tm, d), lambda i, j: (i, 0)),
                pl.BlockSpec((1, d), lambda i, j: (0, 0)),
                pl.BlockSpec((d, tn), lambda i, j: (0, j))]
    out_shape = [jax.ShapeDtypeStruct((n, cols), BF16)]
    out_specs = [pl.BlockSpec((tm, tn), lambda i, j: (i, j))]
    args = [x, g, w]
    if with_lr:
        in_specs.append(pl.BlockSpec((d, LR_COLS), lambda i, j: (0, 0)))
        out_shape.append(jax.ShapeDtypeStruct((n, LR_COLS), F32))
        out_specs.append(pl.BlockSpec((tm, LR_COLS), lambda i, j: (i, 0)))
        args.append(w_lr)
    out = pl.pallas_call(
        functools.partial(_normmm_kernel, with_lr=with_lr),
        grid=(n // tm, cols // tn),
        in_specs=in_specs, out_specs=out_specs, out_shape=out_shape,
        scratch_shapes=[pltpu.VMEM((tm, d), BF16)],
        compiler_params=_cparams(("parallel", "arbitrary")),
        name=name,
    )(*args)
    return out if with_lr else out[0]


def _conv_kernel(ac_ref, bc_ref, ap_ref, bp_ref, an_ref, bn_ref, w_ref, cb_ref, g_ref, be_ref,
                 o_ref, hs_ref, co_ref, *, T):
    i = pl.program_id(1)
    n_t = pl.num_programs(1)
    H = CONV_HALO
    rows = T + 2 * H

    def glu(a_ref, b_ref):
        return a_ref[...].astype(F32) * jax.nn.sigmoid(b_ref[...].astype(F32))

    hs_ref[0, 0:H, :] = jnp.where(i > 0, glu(ap_ref, bp_ref), 0.0)
    hs_ref[0, H:H + T, :] = glu(ac_ref, bc_ref)
    hs_ref[0, H + T:rows, :] = jnp.where(i < n_t - 1, glu(an_ref, bn_ref), 0.0)
    for s in range(1, SUBLANE):
        hs_ref[s, 0:rows - SUBLANE, :] = hs_ref[0, s:s + rows - SUBLANE, :]

    RG = 64
    first = H - CONV_K // 2
    for c in range(D_MODEL // LANE):
        cl = slice(c * LANE, (c + 1) * LANE)
        taps = [w_ref[k * SUBLANE:(k + 1) * SUBLANE, cl] for k in range(CONV_K)]
        bias = jnp.broadcast_to(cb_ref[:, cl], (SUBLANE, LANE))

        def body(g, carry, cl=cl, taps=taps, bias=bias):
            r0 = pl.multiple_of(g * RG, RG)
            for v in range(RG // SUBLANE):
                acc = bias
                for k in range(CONV_K):
                    a, s = divmod(first + k, SUBLANE)
                    acc = acc + hs_ref[s, pl.ds(r0 + (a + v) * SUBLANE, SUBLANE), cl] * taps[k]
                co_ref[pl.ds(r0 + v * SUBLANE, SUBLANE), cl] = acc
            return carry

        lax.fori_loop(0, T // RG, body, 0)

    LG = 32

    def ln_body(g, carry):
        r0 = pl.multiple_of(g * LG, LG)
        hf = co_ref[pl.ds(r0, LG), :]
        mu = jnp.mean(hf, axis=-1, keepdims=True)
        cen = hf - mu
        var = jnp.mean(cen * cen, axis=-1, keepdims=True)
        y = cen * lax.rsqrt(var + EPS) * g_ref[...] + be_ref[...]
        o_ref[pl.ds(r0, LG), :] = (y * jax.nn.sigmoid(y)).astype(o_ref.dtype)
        return carry

    lax.fori_loop(0, T // LG, ln_body, 0)


def _conv_branch(u, dw_w8, dw_b, ln_g, ln_b, *, B, S, T):
    n = B * S
    n_t = S // T
    hb = T // CONV_HALO
    last_hb = n // CONV_HALO - 1
    D = D_MODEL

    def cur(col):
        return pl.BlockSpec((T, D), lambda b, i: (b * n_t + i, col))

    def prev(col):
        return pl.BlockSpec((CONV_HALO, D), lambda b, i: (jnp.maximum((b * n_t + i) * hb - 1, 0), col))

    def nxt(col):
        return pl.BlockSpec((CONV_HALO, D), lambda b, i: (jnp.minimum((b * n_t + i + 1) * hb, last_hb), col))

    def const(shape):
        return pl.BlockSpec(shape, lambda b, i: (0, 0))

    ca, cb = U_A // D, U_B // D
    return pl.pallas_call(
        functools.partial(_conv_kernel, T=T),
        grid=(B, n_t),
        in_specs=[cur(ca), cur(cb), prev(ca), prev(cb), nxt(ca), nxt(cb),
                  const((CONV_K * SUBLANE, D)), const((1, D)), const((1, D)), const((1, D))],
        out_specs=pl.BlockSpec((T, D), lambda b, i: (b * n_t + i, 0)),
        out_shape=jax.ShapeDtypeStruct((n, D), BF16),
        scratch_shapes=[pltpu.VMEM((SUBLANE, T + 2 * CONV_HALO, D), F32), pltpu.VMEM((T, D), F32)],
        compiler_params=_cparams(("parallel", "parallel")),
        name="conv",
    )(u, u, u, u, u, u, dw_w8, dw_b, ln_g, ln_b)


def _gla_kernel(qf_ref, kf_ref, vf_ref, lf_ref, qb_ref, kb_ref, vb_ref, lb_ref, w2_ref, bg_ref,
                of_ref, ob_ref, st_ref, *, C):
    @pl.when(pl.program_id(1) == 0)
    def _():
        st_ref[...] = jnp.zeros_like(st_ref)

    row = lax.broadcasted_iota(I32, (C, C), 0)
    col = lax.broadcasted_iota(I32, (C, C), 1)
    scale = DK_HEAD ** -0.5

    def direction(d, q_ref, k_ref, v_ref, l_ref, o_ref):
        fwd = d == 0
        z = _dot3(l_ref[...], w2_ref[d]) + bg_ref[d]
        la = (jnp.minimum(z, 0.0) - jnp.log1p(jnp.exp(-jnp.abs(z)))) * (1.0 / GATE_TEMP)
        tri = ((row >= col) if fwd else (row <= col)).astype(BF16)
        la_hi, la_lo = _split_bf16(la)
        b = _dot(tri, la_hi) + _dot(tri, la_lo)
        b_far = b[C - 1:C, :] if fwd else b[0:1, :]
        q = q_ref[...].astype(F32) * scale
        k = k_ref[...].astype(F32)
        qs = (q * jnp.exp(b)).astype(BF16)
        ks = (k * jnp.exp(-b)).astype(BF16)
        kd = (k * jnp.exp(b_far - b)).astype(BF16)
        carry_decay = jnp.exp(b_far)
        mask = (row >= col) if fwd else (row < col)
        for h in range(GLA_HEADS):
            kl = slice(h * DK_HEAD, (h + 1) * DK_HEAD)
            vl = slice(h * DV_HEAD, (h + 1) * DV_HEAD)
            attn = _dot(qs[:, kl], ks[:, kl], ((1,), (1,)))
            attn = jnp.where(mask, attn, 0.0).astype(BF16)
            v_h = v_ref[:, vl]
            state_t = st_ref[d, h]
            o_ref[:, vl] = _dot(attn, v_h) + _dot(qs[:, kl], state_t.astype(BF16), ((1,), (1,)))
            st_ref[d, h] = state_t * carry_decay[:, kl] + _dot(v_h, kd[:, kl], ((0,), (0,)))

    direction(0, qf_ref, kf_ref, vf_ref, lf_ref, of_ref)
    direction(1, qb_ref, kb_ref, vb_ref, lb_ref, ob_ref)


def _gla_branch(u, lr, w2p, bg, *, B, S):
    n = B * S
    C = GLA_CHUNK
    n_c = S // C

    def fwd_map(col):
        return lambda b, c: (b * n_c + c, col)

    def bwd_map(col):
        return lambda b, c: (b * n_c + (n_c - 1 - c), col)

    def specs(m):
        return [pl.BlockSpec((C, D_K), m(U_Q // D_K)), pl.BlockSpec((C, D_K), m(U_K // D_K)),
                pl.BlockSpec((C, D_V), m(U_V // D_V)), pl.BlockSpec((C, LR_COLS), m(0))]

    return pl.pallas_call(
        functools.partial(_gla_kernel, C=C),
        grid=(B, n_c),
        in_specs=specs(fwd_map) + specs(bwd_map) + [
            pl.BlockSpec((2, LR_COLS, D_K), lambda b, c: (0, 0, 0)),
            pl.BlockSpec((2, 1, D_K), lambda b, c: (0, 0, 0))],
        out_specs=[pl.BlockSpec((C, D_V), fwd_map(0)), pl.BlockSpec((C, D_V), bwd_map(0))],
        out_shape=[jax.ShapeDtypeStruct((n, D_V), F32), jax.ShapeDtypeStruct((n, D_V), F32)],
        scratch_shapes=[pltpu.VMEM((2, GLA_HEADS, DV_HEAD, DK_HEAD), F32)],
        compiler_params=_cparams(("parallel", "arbitrary")),
        name="gla",
    )(u, u, u, lr, u, u, u, lr, w2p, bg)


def _xattn_kernel(q_ref, k_ref, v_ref, o_ref):
    scale = XA_HEAD ** -0.5
    for h in range(XA_HEADS):
        hl = slice(h * XA_HEAD, (h + 1) * XA_HEAD)
        s = _dot(q_ref[:, hl], k_ref[:, hl], ((1,), (1,))) * scale
        p = jnp.exp(s - jnp.max(s, axis=-1, keepdims=True))
        o = _dot(p.astype(BF16), v_ref[:, hl]) / jnp.sum(p, axis=-1, keepdims=True)
        o_ref[:, hl] = o.astype(o_ref.dtype)


def _xattn(u, kv, *, B, S, M, T):
    n_t = S // T
    D = D_MODEL
    return pl.pallas_call(
        _xattn_kernel,
        grid=(B, n_t),
        in_specs=[pl.BlockSpec((T, D), lambda b, i: (b * n_t + i, U_XQ // D)),
                  pl.BlockSpec((M, D), lambda b, i: (b, 0)),
                  pl.BlockSpec((M, D), lambda b, i: (b, 1))],
        out_specs=pl.BlockSpec((T, D), lambda b, i: (b * n_t + i, 0)),
        out_shape=jax.ShapeDtypeStruct((B * S, D), BF16),
        compiler_params=_cparams(("parallel", "parallel")),
        name="xattn",
    )(u, kv, kv)


def _merge_kernel(ha_ref, of_ref, ob_ref, go_ref, hc_ref, mga_ref, mgb_ref, mgc_ref, x_ref,
                  pw_ref, glao_ref, xao_ref, wout_ref, bm_ref, gng_ref, fng_ref, wr_ref,
                  xo_ref, h2_ref, aff_ref):
    o = of_ref[...] + ob_ref[...]
    parts = []
    for h in range(GLA_HEADS):
        oh = o[:, h * DV_HEAD:(h + 1) * DV_HEAD]
        parts.append(oh * lax.rsqrt(jnp.mean(oh * oh, axis=-1, keepdims=True) + EPS))
    on = jnp.concatenate(parts, axis=-1) * gng_ref[...]
    go = go_ref[...].astype(F32)
    hb = (on * (go * jax.nn.sigmoid(go))).astype(BF16)

    bm = bm_ref[...]
    D = D_MODEL

    def gate(mg_ref, j):
        return jax.nn.sigmoid(mg_ref[...].astype(F32) + bm[:, j * D:(j + 1) * D])

    merged = gate(mga_ref, 0) * _dot(ha_ref[...], pw_ref[...])
    merged = merged + gate(mgb_ref, 1) * _dot(hb, glao_ref[...])
    merged = merged + gate(mgc_ref, 2) * _dot(hc_ref[...], xao_ref[...])
    x_new = x_ref[...] + _dot(merged.astype(BF16), wout_ref[...])
    xo_ref[...] = x_new

    h2 = _rms(x_new, fng_ref[...])
    h2_ref[...] = h2.astype(BF16)
    logits = _dot3(wr_ref[...], h2, ((1,), (1,)))
    e = jnp.exp(logits - jnp.max(logits, axis=0, keepdims=True))
    aff_ref[...] = e / jnp.sum(e, axis=0, keepdims=True)


def _merge(ha, o_f, o_b, u, hc, x, pw, glao, xao, wout, bm, gng, fng, wr_t, *, T):
    n, D = x.shape

    def tile(col=0):
        return pl.BlockSpec((T, D), lambda i: (i, col))

    def const(shape):
        return pl.BlockSpec(shape, lambda i: (0, 0))

    mg0 = U_MG // D
    return pl.pallas_call(
        _merge_kernel,
        grid=(n // T,),
        in_specs=[tile(), tile(), tile(), tile(U_G // D), tile(), tile(mg0), tile(mg0 + 1), tile(mg0 + 2),
                  tile(), const((D, D)), const((D, D)), const((D, D)), const((D, D)),
                  const((1, 3 * D)), const((1, D)), const((1, D)), const((N_EXPERTS, D))],
        out_specs=[tile(), tile(), pl.BlockSpec((N_EXPERTS, T), lambda i: (0, i))],
        out_shape=[jax.ShapeDtypeStruct((n, D), F32), jax.ShapeDtypeStruct((n, D), BF16),
                   jax.ShapeDtypeStruct((N_EXPERTS, n), F32)],
        compiler_params=_cparams(("parallel",)),
        name="merge",
    )(ha, o_f, o_b, u, hc, u, u, u, x, pw, glao, xao, wout, bm, gng, fng, wr_t)


def _route_kernel(a_ref, slot_ref, rowoff_ref, idx_ref, incl_ref, *, R, cap):
    E = N_EXPERTS
    bits = pltpu.bitcast(a_ref[...], I32)

    thr = jnp.zeros((E, 1, 1), I32)
    for bit in range(30, -1, -1):
        cand = thr | (1 << bit)
        cnt = jnp.sum((bits >= cand).astype(I32), axis=(1, 2), keepdims=True)
        thr = jnp.where(cnt >= cap, cand, thr)
    gt = bits > thr
    eq = bits == thr
    need = (cap - jnp.sum(gt.astype(I32), axis=(1, 2), keepdims=True)).astype(F32)

    li0 = lax.broadcasted_iota(I32, (LANE, LANE), 0)
    li1 = lax.broadcasted_iota(I32, (LANE, LANE), 1)
    upper = (li0 <= li1).astype(BF16)
    ri0 = lax.broadcasted_iota(I32, (R, R), 0)
    ri1 = lax.broadcasted_iota(I32, (R, R), 1)
    lower_strict = (ri0 > ri1).astype(BF16)
    upper_r = (ri0 <= ri1).astype(BF16)

    def prefix(m):
        incl = _dot(m.astype(BF16).reshape(E * R, LANE), upper).reshape(E, R, LANE)
        tot = jnp.broadcast_to(incl[:, :, LANE - 1:LANE], (E, R, LANE)).astype(BF16)
        off = jnp.stack([_dot(lower_strict, tot[e]) for e in range(E)])
        return incl, off

    gtf, eqf = gt.astype(F32), eq.astype(F32)
    incl_gt, off_gt = prefix(gtf)
    incl_eq, off_eq = prefix(eqf)
    excl_eq = incl_eq - eqf + off_eq
    sel = gt | (eq & (excl_eq < need))
    self_ = sel.astype(F32)
    rowoff = off_gt + jnp.minimum(off_eq, need)
    pos = (incl_gt - gtf + off_gt) + jnp.minimum(excl_eq, need)
    slot_ref[...] = jnp.where(sel, pos, -1.0).astype(I32)
    rowoff_ref[...] = rowoff.astype(I32)
    incl_ref[...] = pos - rowoff + self_

    s_col = lax.broadcasted_iota(I32, (cap, R), 0).astype(F32)
    s_col_l = lax.broadcasted_iota(I32, (cap, LANE), 0).astype(F32)
    ones8 = jnp.ones((SUBLANE, LANE), BF16)
    r_lane = lax.broadcasted_iota(I32, (SUBLANE, R), 1).astype(BF16)

    def idx_body(e, carry):
        incl_e = incl_ref[e]
        sel_e = (slot_ref[e] >= 0).astype(BF16)
        off_e = rowoff_ref[e].astype(F32)
        tot_lane = _dot(ones8, sel_e, ((1,), (1,)))
        cum_lane = _dot(tot_lane.astype(BF16), upper_r)
        cum1 = cum_lane[0:1, :]
        start1 = cum1 - tot_lane[0:1, :]
        onehot = ((start1 <= s_col) & (s_col < cum1)).astype(BF16)
        off_hi = jnp.floor(off_e * (1.0 / LANE))
        off_lo = off_e - off_hi * LANE
        off_s = _dot(onehot, off_hi.astype(BF16)) * LANE + _dot(onehot, off_lo.astype(BF16))
        incl_row = _dot(onehot, incl_e.astype(BF16))
        before = (incl_row <= (s_col_l - off_s)).astype(BF16)
        col_lane = _dot(ones8, before, ((1,), (1,)))
        row_lane = _dot(r_lane, onehot, ((1,), (1,)))
        idx_ref[pl.ds(e, 1), :] = (row_lane[0:1, :] * LANE + col_lane[0:1, :]).astype(I32)
        return carry

    lax.fori_loop(0, E, idx_body, 0)


def _route(aff_t, *, n):
    R = n // LANE
    cap = CAPACITY_FACTOR * n // N_EXPERTS
    E = N_EXPERTS
    a3 = aff_t.reshape(E, R, LANE)
    slot, rowoff, idx = pl.pallas_call(
        functools.partial(_route_kernel, R=R, cap=cap),
        out_shape=[jax.ShapeDtypeStruct((E, R, LANE), I32), jax.ShapeDtypeStruct((E, R, LANE), I32),
                   jax.ShapeDtypeStruct((E, cap), I32)],
        scratch_shapes=[pltpu.VMEM((E, R, LANE), F32)],
        compiler_params=pltpu.CompilerParams(vmem_limit_bytes=VMEM_LIMIT),
        name="route",
    )(a3)
    return slot, rowoff[:, :, 0], idx


def _ffn_kernel(x_ref, wg_ref, wu_ref, wd_ref, o_ref, acc_ref):
    f = pl.program_id(2)
    x = x_ref[0]
    g = _dot(x, wg_ref[0])
    up = _dot(x, wu_ref[0])
    hid = (g * jax.nn.sigmoid(g) * up).astype(BF16)
    part = _dot(hid, wd_ref[0])

    @pl.when(f == 0)
    def _():
        acc_ref[...] = part

    @pl.when(f > 0)
    def _():
        acc_ref[...] += part

    @pl.when(f == pl.num_programs(2) - 1)
    def _():
        o_ref[0] = acc_ref[...].astype(o_ref.dtype)


def _expert_ffn(xe, wg, wu, wd, *, tm, tf):
    E, cap, D = xe.shape
    F = wg.shape[2]
    return pl.pallas_call(
        _ffn_kernel,
        grid=(E, cap // tm, F // tf),
        in_specs=[pl.BlockSpec((1, tm, D), lambda e, m, f: (e, m, 0)),
                  pl.BlockSpec((1, D, tf), lambda e, m, f: (e, 0, f)),
                  pl.BlockSpec((1, D, tf), lambda e, m, f: (e, 0, f)),
                  pl.BlockSpec((1, tf, D), lambda e, m, f: (e, f, 0))],
        out_specs=pl.BlockSpec((1, tm, D), lambda e, m, f: (e, m, 0)),
        out_shape=jax.ShapeDtypeStruct((E, cap, D), BF16),
        scratch_shapes=[pltpu.VMEM((tm, D), F32)],
        compiler_params=_cparams(("parallel", "parallel", "arbitrary")),
        name="ffn",
    )(xe, wg, wu, wd)


def _combine_kernel(rowoff_ref, x_ref, aff_ref, slot_ref, ye_ref, fg_ref, o_ref, win_ref, sem_ref,
                    *, cap, final_norm):
    r = pl.program_id(0)
    n_r = pl.num_programs(0)
    E = N_EXPERTS
    W = COMBINE_WIN

    def win_start(rr, e):
        s = (rowoff_ref[e, rr] // BF16_ROWS) * BF16_ROWS
        return pl.multiple_of(jnp.minimum(s, cap - W), BF16_ROWS)

    def copy(rr, e, buf):
        return pltpu.make_async_copy(ye_ref.at[e, pl.ds(win_start(rr, e), W), :],
                                     win_ref.at[buf, e], sem_ref.at[buf, e])

    @pl.when(r == 0)
    def _():
        for e in range(E):
            copy(0, e, 0).start()

    @pl.when(r + 1 < n_r)
    def _():
        for e in range(E):
            copy(r + 1, e, (r + 1) % 2).start()

    buf = r % 2
    lane_w = lax.broadcasted_iota(I32, (LANE, W), 1)
    y = x_ref[...]
    for e in range(E):
        copy(r, e, buf).wait()
        rel = slot_ref[:, e:e + 1] - win_start(r, e)
        onehot = (rel == lane_w).astype(BF16)
        y = y + aff_ref[:, e:e + 1] * _dot(onehot, win_ref[buf, e])
    if final_norm:
        y = _rms(y, fg_ref[...])
    o_ref[...] = y


def _combine(x, aff_tok, slot_tok, rowoff, ye, fg, *, final_norm):
    n, D = x.shape
    E, cap, _ = ye.shape
    R = n // LANE
    grid_spec = pltpu.PrefetchScalarGridSpec(
        num_scalar_prefetch=1,
        grid=(R,),
        in_specs=[pl.BlockSpec((LANE, D), lambda r, ro: (r, 0)),
                  pl.BlockSpec((LANE, E), lambda r, ro: (r, 0)),
                  pl.BlockSpec((LANE, E), lambda r, ro: (r, 0)),
                  pl.BlockSpec(memory_space=pl.ANY),
                  pl.BlockSpec((1, D), lambda r, ro: (0, 0))],
        out_specs=pl.BlockSpec((LANE, D), lambda r, ro: (r, 0)),
        scratch_shapes=[pltpu.VMEM((2, E, COMBINE_WIN, D), BF16), pltpu.SemaphoreType.DMA((2, E))],
    )
    return pl.pallas_call(
        functools.partial(_combine_kernel, cap=cap, final_norm=final_norm),
        grid_spec=grid_spec,
        out_shape=jax.ShapeDtypeStruct((n, D), F32),
        compiler_params=_cparams(("arbitrary",)),
        name="combine",
    )(rowoff, x, aff_tok, slot_tok, ye, fg)


def _run_trunk(x, mem, p):
    B, S, D = x.shape
    M = mem.shape[1]
    n = B * S
    x = x.reshape(n, D)
    mem2 = mem.reshape(B * M, D)
    depth = p["w_main"].shape[0]
    conv_t = min(256, S)
    xa_t = min(512, S)
    for l in range(depth):
        u, lr = _norm_matmul(x, p["norm_mix_g"][l], p["w_main"][l], p["w_lr"][l],
                             tm=min(1024, n), tn=1024, name="inproj")
        ha = _conv_branch(u, p["dw_w8"][l], p["conv_dw_b"][l], p["conv_ln_g"][l], p["conv_ln_b"][l],
                          B=B, S=S, T=conv_t)
        o_f, o_b = _gla_branch(u, lr, p["w2p"][l], p["gla_gate_b"][l], B=B, S=S)
        kv = _norm_matmul(mem2, p["mem_norm_g"][l], p["w_mem_kv"][l], tm=M, tn=1024, name="memkv")
        hc = _xattn(u, kv, B=B, S=S, M=M, T=xa_t)
        x, h2, aff_t = _merge(ha, o_f, o_b, u, hc, x, p["conv_pw_w"][l], p["gla_o"][l], p["xa_o"][l],
                              p["w_out"][l], p["b_merge"][l], p["gla_norm_g"][l], p["norm_ffn_g"][l],
                              p["w_router_t"][l], T=min(256, n))
        slot, rowoff, idx = _route(aff_t, n=n)
        xe = jnp.take(h2, idx.reshape(-1), axis=0).reshape(N_EXPERTS, -1, D)
        cap = xe.shape[1]
        ye = _expert_ffn(xe, p["w_gate_e"][l], p["w_up_e"][l], p["w_down_e"][l],
                         tm=min(1024, cap), tf=512)
        x = _combine(x, aff_t.T, slot.reshape(N_EXPERTS, n).T, rowoff, ye, p["final_norm_g"],
                     final_norm=(l == depth - 1))
    return x.reshape(B, S, D)


def _prepare(norm_mix_g, w_in, b_merge, conv_dw_w, conv_dw_b, conv_ln_g, conv_ln_b, conv_pw_w,
             gla_gate_w2, gla_gate_b, gla_norm_g, gla_o, mem_norm_g, w_mem_kv, xa_o, w_out, norm_ffn_g,
             w_router, w_gate_e, w_up_e, w_down_e, final_norm_g):
    L = w_in.shape[0]
    lr0 = U_XQ
    lr1 = lr0 + 2 * GATE_RANK
    w_main = jnp.concatenate([w_in[:, :, :lr0], w_in[:, :, lr1:]], axis=-1).astype(BF16)
    w_lr = jnp.pad(w_in[:, :, lr0:lr1], ((0, 0), (0, 0), (0, LR_COLS - 2 * GATE_RANK))).astype(BF16)
    w2p = jnp.zeros((L, 2, LR_COLS, D_K), F32)
    w2p = w2p.at[:, 0, 0:GATE_RANK].set(gla_gate_w2[:, 0])
    w2p = w2p.at[:, 1, GATE_RANK:2 * GATE_RANK].set(gla_gate_w2[:, 1])
    row = lambda a: a[:, None, :]
    return dict(
        norm_mix_g=row(norm_mix_g), w_main=w_main, w_lr=w_lr, b_merge=row(b_merge),
        dw_w8=jnp.repeat(conv_dw_w, SUBLANE, axis=1), conv_dw_b=row(conv_dw_b),
        conv_ln_g=row(conv_ln_g), conv_ln_b=row(conv_ln_b), conv_pw_w=conv_pw_w.astype(BF16),
        w2p=w2p, gla_gate_b=gla_gate_b[:, :, None, :], gla_norm_g=row(gla_norm_g), gla_o=gla_o.astype(BF16),
        mem_norm_g=row(mem_norm_g), w_mem_kv=w_mem_kv.astype(BF16), xa_o=xa_o.astype(BF16),
        w_out=w_out.astype(BF16), norm_ffn_g=row(norm_ffn_g), w_router_t=jnp.swapaxes(w_router, 1, 2),
        w_gate_e=w_gate_e.astype(BF16), w_up_e=w_up_e.astype(BF16), w_down_e=w_down_e.astype(BF16),
        final_norm_g=final_norm_g[None, :],
    )


def kernel(x_prompt, x_sample, mem_prompt, mem_sample, norm_mix_g, w_in, b_merge, conv_dw_w, conv_dw_b, conv_ln_g, conv_ln_b, conv_pw_w, gla_gate_w2, gla_gate_b, gla_norm_g, gla_o, mem_norm_g, w_mem_kv, xa_o, w_out, norm_ffn_g, w_router, w_gate_e, w_up_e, w_down_e, final_norm_g):
    p = _prepare(norm_mix_g, w_in, b_merge, conv_dw_w, conv_dw_b, conv_ln_g, conv_ln_b, conv_pw_w,
                 gla_gate_w2, gla_gate_b, gla_norm_g, gla_o, mem_norm_g, w_mem_kv, xa_o, w_out, norm_ffn_g,
                 w_router, w_gate_e, w_up_e, w_down_e, final_norm_g)
    return (_run_trunk(x_prompt, mem_prompt, p), _run_trunk(x_sample, mem_sample, p))
```

```python
import functools

import jax
import jax.numpy as jnp
from jax import lax
from jax.experimental import pallas as pl
from jax.experimental.pallas import tpu as pltpu

F32 = jnp.float32
BF16 = jnp.bfloat16
I32 = jnp.int32

D_MODEL = 1024
CONV_K = 31
CONV_HALO = 16
GLA_HEADS = 4
D_K = 512
D_V = 1024
DK_HEAD = D_K // GLA_HEADS
DV_HEAD = D_V // GLA_HEADS
GATE_RANK = 16
GATE_TEMP = 16.0
XA_HEADS = 4
XA_HEAD = D_MODEL // XA_HEADS
N_EXPERTS = 16
D_FF = 2048
CAPACITY_FACTOR = 2
EPS = 1e-6

LANE = 128
SUBLANE = 8
BF16_ROWS = 16
VMEM_LIMIT = 56 * 1024 * 1024

U_A, U_B, U_Q, U_K, U_V, U_G, U_XQ, U_MG = 0, 1024, 2048, 2560, 3072, 4096, 5120, 6144
U_COLS = 9216
LR_COLS = LANE

GLA_CHUNK = 128
COMBINE_MAX_CHUNKS = LANE // BF16_ROWS + 1
COMBINE_KT = 256


def _cparams(sem):
    return pltpu.CompilerParams(dimension_semantics=sem, vmem_limit_bytes=VMEM_LIMIT)


def _split_bf16(a):
    hi = a.astype(BF16)
    lo = (a - hi.astype(F32)).astype(BF16)
    return hi, lo


def _dot(a, b, dims=None):
    if dims is None:
        return jnp.dot(a, b, preferred_element_type=F32)
    return lax.dot_general(a, b, (dims, ((), ())), preferred_element_type=F32)


def _dot3(a, b, dims=None):
    ah, al = _split_bf16(a)
    bh, bl = _split_bf16(b)
    return _dot(ah, bh, dims) + (_dot(al, bh, dims) + _dot(ah, bl, dims))


def _rms(x, g):
    return x * lax.rsqrt(jnp.mean(x * x, axis=-1, keepdims=True) + EPS) * g


def _normmm_kernel(x_ref, g_ref, w_ref, *rest, with_lr):
    if with_lr:
        wlr_ref, u_ref, lr_ref, hn_ref = rest
    else:
        u_ref, hn_ref = rest

    @pl.when(pl.program_id(1) == 0)
    def _():
        hn = _rms(x_ref[...], g_ref[...]).astype(BF16)
        hn_ref[...] = hn
        if with_lr:
            lr_ref[...] = _dot(hn, wlr_ref[...])

    u_ref[...] = _dot(hn_ref[...], w_ref[...]).astype(u_ref.dtype)


def _norm_matmul(x, g, w, w_lr=None, *, tm, tn, name):
    n, d = x.shape
    cols = w.shape[1]
    with_lr = w_lr is not None
    in_specs = [pl.BlockSpec((tm, d), lambda i, j: (i, 0)),
                pl.BlockSpec((1, d), lambda i, j: (0, 0)),
                pl.BlockSpec((d, tn), lambda i, j: (0, j))]
    out_shape = [jax.ShapeDtypeStruct((n, cols), BF16)]
    out_specs = [pl.BlockSpec((tm, tn), lambda i, j: (i, j))]
    args = [x, g, w]
    if with_lr:
        in_specs.append(pl.BlockSpec((d, LR_COLS), lambda i, j: (0, 0)))
        out_shape.append(jax.ShapeDtypeStruct((n, LR_COLS), F32))
        out_specs.append(pl.BlockSpec((tm, LR_COLS), lambda i, j: (i, 0)))
        args.append(w_lr)
    out = pl.pallas_call(
        functools.partial(_normmm_kernel, with_lr=with_lr),
        grid=(n // tm, cols // tn),
        in_specs=in_specs, out_specs=out_specs, out_shape=out_shape,
        scratch_shapes=[pltpu.VMEM((tm, d), BF16)],
        compiler_params=_cparams(("parallel", "arbitrary")),
        name=name,
    )(*args)
    return out if with_lr else out[0]


def _conv_kernel(ac_ref, bc_ref, ap_ref, bp_ref, an_ref, bn_ref, w_ref, cb_ref, g_ref, be_ref,
                 o_ref, hs_ref, co_ref, *, T):
    i = pl.program_id(1)
    n_t = pl.num_programs(1)
    H = CONV_HALO
    rows = T + 2 * H

    def glu(a_ref, b_ref):
        return a_ref[...].astype(F32) * jax.nn.sigmoid(b_ref[...].astype(F32))

    hs_ref[0, 0:H, :] = jnp.where(i > 0, glu(ap_ref, bp_ref), 0.0)
    hs_ref[0, H:H + T, :] = glu(ac_ref, bc_ref)
    hs_ref[0, H + T:rows, :] = jnp.where(i < n_t - 1, glu(an_ref, bn_ref), 0.0)
    for s in range(1, SUBLANE):
        hs_ref[s, 0:rows - SUBLANE, :] = hs_ref[0, s:s + rows - SUBLANE, :]

    RG = 64
    first = H - CONV_K // 2
    for c in range(D_MODEL // LANE):
        cl = slice(c * LANE, (c + 1) * LANE)
        taps = [w_ref[k * SUBLANE:(k + 1) * SUBLANE, cl] for k in range(CONV_K)]
        bias = jnp.broadcast_to(cb_ref[:, cl], (SUBLANE, LANE))

        def body(g, carry, cl=cl, taps=taps, bias=bias):
            r0 = pl.multiple_of(g * RG, RG)
            for v in range(RG // SUBLANE):
                acc = bias
                for k in range(CONV_K):
                    a, s = divmod(first + k, SUBLANE)
                    acc = acc + hs_ref[s, pl.ds(r0 + (a + v) * SUBLANE, SUBLANE), cl] * taps[k]
                co_ref[pl.ds(r0 + v * SUBLANE, SUBLANE), cl] = acc
            return carry

        lax.fori_loop(0, T // RG, body, 0)

    LG = 64
    for r0 in range(0, T, LG):
        hf = co_ref[r0:r0 + LG, :]
        mu = jnp.mean(hf, axis=-1, keepdims=True)
        cen = hf - mu
        var = jnp.mean(cen * cen, axis=-1, keepdims=True)
        y = cen * lax.rsqrt(var + EPS) * g_ref[...] + be_ref[...]
        o_ref[r0:r0 + LG, :] = (y * jax.nn.sigmoid(y)).astype(o_ref.dtype)


def _conv_branch(u, dw_w8, dw_b, ln_g, ln_b, *, B, S, T):
    n = B * S
    n_t = S // T
    hb = T // CONV_HALO
    last_hb = n // CONV_HALO - 1
    D = D_MODEL

    def cur(col):
        return pl.BlockSpec((T, D), lambda b, i: (b * n_t + i, col))

    def prev(col):
        return pl.BlockSpec((CONV_HALO, D), lambda b, i: (jnp.maximum((b * n_t + i) * hb - 1, 0), col))

    def nxt(col):
        return pl.BlockSpec((CONV_HALO, D), lambda b, i: (jnp.minimum((b * n_t + i + 1) * hb, last_hb), col))

    def const(shape):
        return pl.BlockSpec(shape, lambda b, i: (0, 0))

    ca, cb = U_A // D, U_B // D
    return pl.pallas_call(
        functools.partial(_conv_kernel, T=T),
        grid=(B, n_t),
        in_specs=[cur(ca), cur(cb), prev(ca), prev(cb), nxt(ca), nxt(cb),
                  const((CONV_K * SUBLANE, D)), const((1, D)), const((1, D)), const((1, D))],
        out_specs=pl.BlockSpec((T, D), lambda b, i: (b * n_t + i, 0)),
        out_shape=jax.ShapeDtypeStruct((n, D), BF16),
        scratch_shapes=[pltpu.VMEM((SUBLANE, T + 2 * CONV_HALO, D), F32), pltpu.VMEM((T, D), F32)],
        compiler_params=_cparams(("parallel", "parallel")),
        name="conv",
    )(u, u, u, u, u, u, dw_w8, dw_b, ln_g, ln_b)


def _gla_kernel(qf_ref, kf_ref, vf_ref, lf_ref, qb_ref, kb_ref, vb_ref, lb_ref, w2_ref, bg_ref,
                of_ref, ob_ref, st_ref, *, C):
    @pl.when(pl.program_id(1) == 0)
    def _():
        st_ref[...] = jnp.zeros_like(st_ref)

    row = lax.broadcasted_iota(I32, (C, C), 0)
    col = lax.broadcasted_iota(I32, (C, C), 1)
    scale = DK_HEAD ** -0.5

    def direction(d, q_ref, k_ref, v_ref, l_ref, o_ref):
        fwd = d == 0
        z = _dot3(l_ref[...], w2_ref[d]) + bg_ref[d]
        la = (jnp.minimum(z, 0.0) - jnp.log1p(jnp.exp(-jnp.abs(z)))) * (1.0 / GATE_TEMP)
        tri = ((row >= col) if fwd else (row <= col)).astype(BF16)
        la_hi, la_lo = _split_bf16(la)
        b = _dot(tri, la_hi) + _dot(tri, la_lo)
        b_far = b[C - 1:C, :] if fwd else b[0:1, :]
        q = q_ref[...].astype(F32) * scale
        k = k_ref[...].astype(F32)
        qs = (q * jnp.exp(b)).astype(BF16)
        ks = (k * jnp.exp(-b)).astype(BF16)
        kd = (k * jnp.exp(b_far - b)).astype(BF16)
        carry_decay = jnp.exp(b_far)
        mask = (row >= col) if fwd else (row < col)
        for h in range(GLA_HEADS):
            kl = slice(h * DK_HEAD, (h + 1) * DK_HEAD)
            vl = slice(h * DV_HEAD, (h + 1) * DV_HEAD)
            attn = _dot(qs[:, kl], ks[:, kl], ((1,), (1,)))
            attn = jnp.where(mask, attn, 0.0).astype(BF16)
            v_h = v_ref[:, vl]
            state_t = st_ref[d, h]
            o_ref[:, vl] = _dot(attn, v_h) + _dot(qs[:, kl], state_t.astype(BF16), ((1,), (1,)))
            st_ref[d, h] = state_t * carry_decay[:, kl] + _dot(v_h, kd[:, kl], ((0,), (0,)))

    direction(0, qf_ref, kf_ref, vf_ref, lf_ref, of_ref)
    direction(1, qb_ref, kb_ref, vb_ref, lb_ref, ob_ref)


def _gla_branch(u, lr, w2p, bg, *, B, S):
    n = B * S
    C = GLA_CHUNK
    n_c = S // C

    def fwd_map(col):
        return lambda b, c: (b * n_c + c, col)

    def bwd_map(col):
        return lambda b, c: (b * n_c + (n_c - 1 - c), col)

    def specs(m):
        return [pl.BlockSpec((C, D_K), m(U_Q // D_K)), pl.BlockSpec((C, D_K), m(U_K // D_K)),
                pl.BlockSpec((C, D_V), m(U_V // D_V)), pl.BlockSpec((C, LR_COLS), m(0))]

    return pl.pallas_call(
        functools.partial(_gla_kernel, C=C),
        grid=(B, n_c),
        in_specs=specs(fwd_map) + specs(bwd_map) + [
            pl.BlockSpec((2, LR_COLS, D_K), lambda b, c: (0, 0, 0)),
            pl.BlockSpec((2, 1, D_K), lambda b, c: (0, 0, 0))],
        out_specs=[pl.BlockSpec((C, D_V), fwd_map(0)), pl.BlockSpec((C, D_V), bwd_map(0))],
        out_shape=[jax.ShapeDtypeStruct((n, D_V), F32), jax.ShapeDtypeStruct((n, D_V), F32)],
        scratch_shapes=[pltpu.VMEM((2, GLA_HEADS, DV_HEAD, DK_HEAD), F32)],
        compiler_params=_cparams(("parallel", "arbitrary")),
        name="gla",
    )(u, u, u, lr, u, u, u, lr, w2p, bg)


def _xattn_kernel(q_ref, k_ref, v_ref, o_ref):
    scale = XA_HEAD ** -0.5
    for h in range(XA_HEADS):
        hl = slice(h * XA_HEAD, (h + 1) * XA_HEAD)
        s = _dot(q_ref[:, hl], k_ref[:, hl], ((1,), (1,))) * scale
        p = jnp.exp(s - jnp.max(s, axis=-1, keepdims=True))
        o = _dot(p.astype(BF16), v_ref[:, hl]) / jnp.sum(p, axis=-1, keepdims=True)
        o_ref[:, hl] = o.astype(o_ref.dtype)


def _xattn(u, kv, *, B, S, M, T):
    n_t = S // T
    D = D_MODEL
    return pl.pallas_call(
        _xattn_kernel,
        grid=(B, n_t),
        in_specs=[pl.BlockSpec((T, D), lambda b, i: (b * n_t + i, U_XQ // D)),
                  pl.BlockSpec((M, D), lambda b, i: (b, 0)),
                  pl.BlockSpec((M, D), lambda b, i: (b, 1))],
        out_specs=pl.BlockSpec((T, D), lambda b, i: (b * n_t + i, 0)),
        out_shape=jax.ShapeDtypeStruct((B * S, D), BF16),
        compiler_params=_cparams(("parallel", "parallel")),
        name="xattn",
    )(u, kv, kv)


def _merge_kernel(ha_ref, of_ref, ob_ref, go_ref, hc_ref, mga_ref, mgb_ref, mgc_ref, x_ref,
                  pw_ref, glao_ref, xao_ref, wout_ref, bm_ref, gng_ref, fng_ref, wr_ref,
                  xo_ref, aff_ref):
    o = of_ref[...] + ob_ref[...]
    parts = []
    for h in range(GLA_HEADS):
        oh = o[:, h * DV_HEAD:(h + 1) * DV_HEAD]
        parts.append(oh * lax.rsqrt(jnp.mean(oh * oh, axis=-1, keepdims=True) + EPS))
    on = jnp.concatenate(parts, axis=-1) * gng_ref[...]
    go = go_ref[...].astype(F32)
    hb = (on * (go * jax.nn.sigmoid(go))).astype(BF16)

    bm = bm_ref[...]
    D = D_MODEL

    def gate(mg_ref, j):
        return jax.nn.sigmoid(mg_ref[...].astype(F32) + bm[:, j * D:(j + 1) * D])

    merged = gate(mga_ref, 0) * _dot(ha_ref[...], pw_ref[...])
    merged = merged + gate(mgb_ref, 1) * _dot(hb, glao_ref[...])
    merged = merged + gate(mgc_ref, 2) * _dot(hc_ref[...], xao_ref[...])
    x_new = x_ref[...] + _dot(merged.astype(BF16), wout_ref[...])
    xo_ref[...] = x_new

    h2 = _rms(x_new, fng_ref[...])
    logits = _dot3(wr_ref[...], h2, ((1,), (1,)))
    e = jnp.exp(logits - jnp.max(logits, axis=0, keepdims=True))
    aff_ref[...] = e / jnp.sum(e, axis=0, keepdims=True)


def _merge(ha, o_f, o_b, u, hc, x, pw, glao, xao, wout, bm, gng, fng, wr_t, *, T):
    n, D = x.shape

    def tile(col=0):
        return pl.BlockSpec((T, D), lambda i: (i, col))

    def const(shape):
        return pl.BlockSpec(shape, lambda i: (0, 0))

    mg0 = U_MG // D
    return pl.pallas_call(
        _merge_kernel,
        grid=(n // T,),
        in_specs=[tile(), tile(), tile(), tile(U_G // D), tile(), tile(mg0), tile(mg0 + 1), tile(mg0 + 2),
                  tile(), const((D, D)), const((D, D)), const((D, D)), const((D, D)),
                  const((1, 3 * D)), const((1, D)), const((1, D)), const((N_EXPERTS, D))],
        out_specs=[tile(), pl.BlockSpec((N_EXPERTS, T), lambda i: (0, i))],
        out_shape=[jax.ShapeDtypeStruct((n, D), F32), jax.ShapeDtypeStruct((N_EXPERTS, n), F32)],
        compiler_params=_cparams(("parallel",)),
        name="merge",
    )(ha, o_f, o_b, u, hc, u, u, u, x, pw, glao, xao, wout, bm, gng, fng, wr_t)


def _route_kernel(a_ref, slot_ref, rowoff_ref, idx_ref, gate_ref, incl_ref, *, R, cap):
    E = N_EXPERTS
    bits = pltpu.bitcast(a_ref[...], I32)

    thr = jnp.zeros((E, 1, 1), I32)
    for bit in range(30, -1, -1):
        cand = thr | (1 << bit)
        cnt = jnp.sum((bits >= cand).astype(I32), axis=(1, 2), keepdims=True)
        thr = jnp.where(cnt >= cap, cand, thr)
    gt = bits > thr
    eq = bits == thr
    need = (cap - jnp.sum(gt.astype(I32), axis=(1, 2), keepdims=True)).astype(F32)

    li0 = lax.broadcasted_iota(I32, (LANE, LANE), 0)
    li1 = lax.broadcasted_iota(I32, (LANE, LANE), 1)
    upper = (li0 <= li1).astype(BF16)
    ri0 = lax.broadcasted_iota(I32, (R, R), 0)
    ri1 = lax.broadcasted_iota(I32, (R, R), 1)
    lower_strict = (ri0 > ri1).astype(BF16)
    upper_r = (ri0 <= ri1).astype(BF16)

    def prefix(m):
        incl = _dot(m.astype(BF16).reshape(E * R, LANE), upper).reshape(E, R, LANE)
        tot = jnp.broadcast_to(incl[:, :, LANE - 1:LANE], (E, R, LANE)).astype(BF16)
        off = jnp.stack([_dot(lower_strict, tot[e]) for e in range(E)])
        return incl, off

    gtf, eqf = gt.astype(F32), eq.astype(F32)
    incl_gt, off_gt = prefix(gtf)
    incl_eq, off_eq = prefix(eqf)
    excl_eq = incl_eq - eqf + off_eq
    sel = gt | (eq & (excl_eq < need))
    self_ = sel.astype(F32)
    rowoff = off_gt + jnp.minimum(off_eq, need)
    pos = (incl_gt - gtf + off_gt) + jnp.minimum(excl_eq, need)
    slot_ref[...] = jnp.where(sel, pos, -1.0).astype(I32)
    rowoff_ref[...] = rowoff.astype(I32)
    incl_ref[...] = pos - rowoff + self_

    s_col = lax.broadcasted_iota(I32, (cap, R), 0).astype(F32)
    s_col_l = lax.broadcasted_iota(I32, (cap, LANE), 0).astype(F32)
    ones8 = jnp.ones((SUBLANE, LANE), BF16)
    r_lane = lax.broadcasted_iota(I32, (SUBLANE, R), 1).astype(BF16)

    def idx_body(e, carry):
        incl_e = incl_ref[e]
        sel_e = (slot_ref[e] >= 0).astype(BF16)
        off_e = rowoff_ref[e].astype(F32)
        tot_lane = _dot(ones8, sel_e, ((1,), (1,)))
        cum_lane = _dot(tot_lane.astype(BF16), upper_r)
        cum1 = cum_lane[0:1, :]
        start1 = cum1 - tot_lane[0:1, :]
        onehot = ((start1 <= s_col) & (s_col < cum1)).astype(BF16)
        off_hi = jnp.floor(off_e * (1.0 / LANE))
        off_lo = off_e - off_hi * LANE
        off_s = _dot(onehot, off_hi.astype(BF16)) * LANE + _dot(onehot, off_lo.astype(BF16))
        incl_row = _dot(onehot, incl_e.astype(BF16))
        rank = s_col_l - off_s
        before = (incl_row <= rank).astype(BF16)
        col_lane = _dot(ones8, before, ((1,), (1,)))
        row_lane = _dot(r_lane, onehot, ((1,), (1,)))
        idx_ref[pl.ds(e, 1), :] = (row_lane[0:1, :] * LANE + col_lane[0:1, :]).astype(I32)
        a_e = a_ref[e]
        a_hi = a_e.astype(BF16)
        a_mid = (a_e - a_hi.astype(F32)).astype(BF16)
        a_lo = (a_e - a_hi.astype(F32) - a_mid.astype(F32)).astype(BF16)
        aff_row = _dot(onehot, a_hi) + (_dot(onehot, a_mid) + _dot(onehot, a_lo))
        at_col = (incl_row == rank + 1.0) & (_dot(onehot, sel_e) > 0.5)
        gate = jnp.sum(jnp.where(at_col, aff_row, 0.0), axis=-1, keepdims=True)
        gate_ref[e] = jnp.broadcast_to(gate, (cap, LANE))
        return carry

    lax.fori_loop(0, E, idx_body, 0)


def _route(aff_t, *, n):
    R = n // LANE
    cap = CAPACITY_FACTOR * n // N_EXPERTS
    E = N_EXPERTS
    a3 = aff_t.reshape(E, R, LANE)
    slot, rowoff, idx, gate = pl.pallas_call(
        functools.partial(_route_kernel, R=R, cap=cap),
        out_shape=[jax.ShapeDtypeStruct((E, R, LANE), I32), jax.ShapeDtypeStruct((E, R, LANE), I32),
                   jax.ShapeDtypeStruct((E, cap), I32), jax.ShapeDtypeStruct((E, cap, LANE), F32)],
        scratch_shapes=[pltpu.VMEM((E, R, LANE), F32)],
        compiler_params=pltpu.CompilerParams(vmem_limit_bytes=VMEM_LIMIT),
        name="route",
    )(a3)
    return slot, rowoff[:, :, 0], idx, gate


def _gather_kernel(idx_ref, x_ref, o_ref, sem_ref, *, rows):
    base = pl.program_id(0) * rows

    def issue(i, carry):
        dst = base + i
        pltpu.make_async_copy(x_ref.at[pl.ds(idx_ref[dst], 1), :], o_ref.at[pl.ds(dst, 1), :], sem_ref).start()
        return carry

    lax.fori_loop(0, rows, issue, 0)
    pltpu.make_async_copy(x_ref.at[pl.ds(0, rows), :], o_ref.at[pl.ds(base, rows), :], sem_ref).wait()


def _gather_rows(x, idx, *, rows):
    total = idx.shape[0]
    D = x.shape[1]
    return pl.pallas_call(
        functools.partial(_gather_kernel, rows=rows),
        grid_spec=pltpu.PrefetchScalarGridSpec(
            num_scalar_prefetch=1, grid=(total // rows,),
            in_specs=[pl.BlockSpec(memory_space=pl.ANY)],
            out_specs=pl.BlockSpec(memory_space=pl.ANY),
            scratch_shapes=[pltpu.SemaphoreType.DMA(())]),
        out_shape=jax.ShapeDtypeStruct((total, D), x.dtype),
        compiler_params=_cparams(("arbitrary",)),
        name="gather",
    )(idx, x)


def _ffn_kernel(x_ref, ng_ref, gate_ref, wg_ref, wu_ref, wd_ref, o_ref, hn_ref, acc_ref):
    f = pl.program_id(2)

    @pl.when(f == 0)
    def _():
        hn_ref[...] = _rms(x_ref[0], ng_ref[...]).astype(BF16)

    x = hn_ref[...]
    g = _dot(x, wg_ref[0])
    up = _dot(x, wu_ref[0])
    hid = (g * jax.nn.sigmoid(g) * up).astype(BF16)
    part = _dot(hid, wd_ref[0])

    @pl.when(f == 0)
    def _():
        acc_ref[...] = part

    @pl.when(f > 0)
    def _():
        acc_ref[...] += part

    @pl.when(f == pl.num_programs(2) - 1)
    def _():
        gate = gate_ref[0]
        for c in range(D_MODEL // LANE):
            cl = slice(c * LANE, (c + 1) * LANE)
            o_ref[0, :, cl] = (acc_ref[:, cl] * gate).astype(o_ref.dtype)


def _expert_ffn(xe, ng, gate, wg, wu, wd, *, tm, tf):
    E, cap, D = xe.shape
    F = wg.shape[2]
    return pl.pallas_call(
        _ffn_kernel,
        grid=(E, cap // tm, F // tf),
        in_specs=[pl.BlockSpec((1, tm, D), lambda e, m, f: (e, m, 0)),
                  pl.BlockSpec((1, D), lambda e, m, f: (0, 0)),
                  pl.BlockSpec((1, tm, LANE), lambda e, m, f: (e, m, 0)),
                  pl.BlockSpec((1, D, tf), lambda e, m, f: (e, 0, f)),
                  pl.BlockSpec((1, D, tf), lambda e, m, f: (e, 0, f)),
                  pl.BlockSpec((1, tf, D), lambda e, m, f: (e, f, 0))],
        out_specs=pl.BlockSpec((1, tm, D), lambda e, m, f: (e, m, 0)),
        out_shape=jax.ShapeDtypeStruct((E, cap, D), BF16),
        scratch_shapes=[pltpu.VMEM((tm, D), BF16), pltpu.VMEM((tm, D), F32)],
        compiler_params=_cparams(("parallel", "parallel", "arbitrary")),
        name="ffn",
    )(xe, ng, gate, wg, wu, wd)


def _combine_plan(slot, rowoff, *, cap):
    E, R = rowoff.shape
    CH = BF16_ROWS
    nxt = jnp.concatenate([rowoff[:, 1:], jnp.full((E, 1), cap, I32)], axis=1)
    cnt = nxt - rowoff
    c0 = rowoff // CH
    nch = jnp.where(cnt > 0, (nxt + CH - 1) // CH - c0, 0)
    cum = jnp.cumsum(nch, axis=0)
    total = cum[-1]
    kc = E * COMBINE_MAX_CHUNKS
    p = jnp.arange(kc, dtype=I32)
    e_of = jnp.minimum(jnp.sum((cum[:, :, None] <= p[None, None, :]).astype(I32), axis=0), E - 1)
    e_is = e_of[None, :, :] == jnp.arange(E, dtype=I32)[:, None, None]
    pick = lambda a: jnp.sum(jnp.where(e_is, a[:, :, None], 0), axis=0)
    in_expert = pick(c0) + (p[None, :] - pick(cum - nch))
    valid = p[None, :] < total[:, None]
    chunk = jnp.where(valid, e_of * (cap // CH) + in_expert, 0)
    row_expert = jnp.broadcast_to(jnp.where(valid, e_of, -1)[:, :, None], (R, kc, CH))
    row_slot = jnp.where(valid[:, :, None], in_expert[:, :, None] * CH + jnp.arange(CH, dtype=I32), -2).astype(F32)
    return (chunk.reshape(-1), total, row_expert.reshape(R, 1, kc * CH), row_slot.reshape(R, 1, kc * CH),
            slot.reshape(E, -1).T)


def _combine_kernel(chunks_ref, total_ref, x_ref, slot_ref, rowe_ref, rows_ref, ye_ref, fg_ref, o_ref,
                    buf_ref, sem_ref, *, kc, final_norm):
    r = pl.program_id(0)
    n_r = pl.num_programs(0)
    CH = BF16_ROWS
    KT = COMBINE_KT
    KTC = KT // CH

    def start_all(rr, b):
        def issue(i, c):
            pltpu.make_async_copy(ye_ref.at[chunks_ref[rr * kc + i]], buf_ref.at[b, i], sem_ref.at[b]).start()
            return c

        lax.fori_loop(0, total_ref[rr], issue, 0)

    @pl.when(r == 0)
    def _():
        buf_ref[...] = jnp.zeros_like(buf_ref)
        start_all(0, 0)

    @pl.when(r + 1 < n_r)
    def _():
        start_all(r + 1, (r + 1) % 2)

    b = r % 2
    n_chunks = total_ref[r]
    for bit in range(kc.bit_length()):
        @pl.when(((n_chunks >> bit) & 1) == 1)
        def _(m=1 << bit):
            pltpu.make_async_copy(ye_ref.at[pl.ds(0, m)], buf_ref.at[b, pl.ds(0, m)], sem_ref.at[b]).wait()

    SPLIT = 64
    slot = slot_ref[...]
    slot_hi = (slot >> 6).astype(F32).astype(BF16)
    slot_lo = (slot & (SPLIT - 1)).astype(F32).astype(BF16)
    expert_iota = lax.broadcasted_iota(I32, (N_EXPERTS, KT), 0)
    o_ref[...] = x_ref[...]

    def k_body(kt, carry):
        k0 = pl.multiple_of(kt * KT, KT)
        of_expert = (expert_iota == rowe_ref[0, :, pl.ds(k0, KT)]).astype(F32).astype(BF16)
        wanted = _dot(slot_hi, of_expert) * SPLIT + _dot(slot_lo, of_expert)
        onehot = jnp.where(wanted == rows_ref[0, :, pl.ds(k0, KT)], 1.0, 0.0).astype(BF16)
        rows = buf_ref[b, pl.ds(pl.multiple_of(kt * KTC, KTC), KTC)].reshape(KT, D_MODEL)
        o_ref[...] += _dot(onehot, rows)
        return carry

    lax.fori_loop(0, (total_ref[r] * CH + KT - 1) // KT, k_body, 0)
    if final_norm:
        o_ref[...] = _rms(o_ref[...], fg_ref[...])


def _combine(x, ye, plan, fg, *, final_norm):
    n, D = x.shape
    E, cap, _ = ye.shape
    R = n // LANE
    chunks, total, row_expert, row_slot, slot_tok = plan
    kc = E * COMBINE_MAX_CHUNKS
    ye_chunks = ye.reshape(E * cap // BF16_ROWS, BF16_ROWS, D)
    row_spec = pl.BlockSpec((1, 1, kc * BF16_ROWS), lambda r, *_: (r, 0, 0))
    grid_spec = pltpu.PrefetchScalarGridSpec(
        num_scalar_prefetch=2,
        grid=(R,),
        in_specs=[pl.BlockSpec((LANE, D), lambda r, *_: (r, 0)),
                  pl.BlockSpec((LANE, E), lambda r, *_: (r, 0)),
                  row_spec, row_spec,
                  pl.BlockSpec(memory_space=pl.ANY),
                  pl.BlockSpec((1, D), lambda r, *_: (0, 0))],
        out_specs=pl.BlockSpec((LANE, D), lambda r, *_: (r, 0)),
        scratch_shapes=[pltpu.VMEM((2, kc, BF16_ROWS, D), BF16), pltpu.SemaphoreType.DMA((2,))],
    )
    return pl.pallas_call(
        functools.partial(_combine_kernel, kc=kc, final_norm=final_norm),
        grid_spec=grid_spec,
        out_shape=jax.ShapeDtypeStruct((n, D), F32),
        compiler_params=_cparams(("arbitrary",)),
        name="combine",
    )(chunks, total, x, slot_tok, row_expert, row_slot, ye_chunks, fg)


def _run_trunk(x, mem, p):
    B, S, D = x.shape
    M = mem.shape[1]
    n = B * S
    cap = CAPACITY_FACTOR * n // N_EXPERTS
    x = x.reshape(n, D)
    mem2 = mem.reshape(B * M, D)
    depth = p["w_main"].shape[0]
    conv_t = min(256, S)
    xa_t = min(512, S)
    for l in range(depth):
        u, lr = _norm_matmul(x, p["norm_mix_g"][l], p["w_main"][l], p["w_lr"][l],
                             tm=min(1024, n), tn=1024, name="inproj")
        ha = _conv_branch(u, p["dw_w8"][l], p["conv_dw_b"][l], p["conv_ln_g"][l], p["conv_ln_b"][l],
                          B=B, S=S, T=conv_t)
        o_f, o_b = _gla_branch(u, lr, p["w2p"][l], p["gla_gate_b"][l], B=B, S=S)
        kv = _norm_matmul(mem2, p["mem_norm_g"][l], p["w_mem_kv"][l], tm=M, tn=1024, name="memkv")
        hc = _xattn(u, kv, B=B, S=S, M=M, T=xa_t)
        x, aff_t = _merge(ha, o_f, o_b, u, hc, x, p["conv_pw_w"][l], p["gla_o"][l], p["xa_o"][l],
                          p["w_out"][l], p["b_merge"][l], p["gla_norm_g"][l], p["norm_ffn_g"][l],
                          p["w_router_t"][l], T=min(256, n))
        slot, rowoff, idx, gate = _route(aff_t, n=n)
        xe = _gather_rows(x, idx.reshape(-1), rows=min(2048, cap)).reshape(N_EXPERTS, cap, D)
        ye = _expert_ffn(xe, p["norm_ffn_g"][l], gate, p["w_gate_e"][l], p["w_up_e"][l], p["w_down_e"][l],
                         tm=min(1024, cap), tf=512)
        x = _combine(x, ye, _combine_plan(slot, rowoff, cap=cap), p["final_norm_g"],
                     final_norm=(l == depth - 1))
    return x.reshape(B, S, D)


def _prepare(norm_mix_g, w_in, b_merge, conv_dw_w, conv_dw_b, conv_ln_g, conv_ln_b, conv_pw_w,
             gla_gate_w2, gla_gate_b, gla_norm_g, gla_o, mem_norm_g, w_mem_kv, xa_o, w_out, norm_ffn_g,
             w_router, w_gate_e, w_up_e, w_down_e, final_norm_g):
    L = w_in.shape[0]
    lr0 = U_XQ
    lr1 = lr0 + 2 * GATE_RANK
    w_main = jnp.concatenate([w_in[:, :, :lr0], w_in[:, :, lr1:]], axis=-1).astype(BF16)
    w_lr = jnp.pad(w_in[:, :, lr0:lr1], ((0, 0), (0, 0), (0, LR_COLS - 2 * GATE_RANK))).astype(BF16)
    w2p = jnp.zeros((L, 2, LR_COLS, D_K), F32)
    w2p = w2p.at[:, 0, 0:GATE_RANK].set(gla_gate_w2[:, 0])
    w2p = w2p.at[:, 1, GATE_RANK:2 * GATE_RANK].set(gla_gate_w2[:, 1])
    row = lambda a: a[:, None, :]
    return dict(
        norm_mix_g=row(norm_mix_g), w_main=w_main, w_lr=w_lr, b_merge=row(b_merge),
        dw_w8=jnp.repeat(conv_dw_w, SUBLANE, axis=1), conv_dw_b=row(conv_dw_b),
        conv_ln_g=row(conv_ln_g), conv_ln_b=row(conv_ln_b), conv_pw_w=conv_pw_w.astype(BF16),
        w2p=w2p, gla_gate_b=gla_gate_b[:, :, None, :], gla_norm_g=row(gla_norm_g), gla_o=gla_o.astype(BF16),
        mem_norm_g=row(mem_norm_g), w_mem_kv=w_mem_kv.astype(BF16), xa_o=xa_o.astype(BF16),
        w_out=w_out.astype(BF16), norm_ffn_g=row(norm_ffn_g), w_router_t=jnp.swapaxes(w_router, 1, 2),
        w_gate_e=w_gate_e.astype(BF16), w_up_e=w_up_e.astype(BF16), w_down_e=w_down_e.astype(BF16),
        final_norm_g=final_norm_g[None, :],
    )


def kernel(x_prompt, x_sample, mem_prompt, mem_sample, norm_mix_g, w_in, b_merge, conv_dw_w, conv_dw_b, conv_ln_g, conv_ln_b, conv_pw_w, gla_gate_w2, gla_gate_b, gla_norm_g, gla_o, mem_norm_g, w_mem_kv, xa_o, w_out, norm_ffn_g, w_router, w_gate_e, w_up_e, w_down_e, final_norm_g):
    p = _prepare(norm_mix_g, w_in, b_merge, conv_dw_w, conv_dw_b, conv_ln_g, conv_ln_b, conv_pw_w,
                 gla_gate_w2, gla_gate_b, gla_norm_g, gla_o, mem_norm_g, w_mem_kv, xa_o, w_out, norm_ffn_g,
                 w_router, w_gate_e, w_up_e, w_down_e, final_norm_g)
    return (_run_trunk(x_prompt, mem_prompt, p), _run_trunk(x_sample, mem_sample, p))
```

```python
import functools

import jax
import jax.numpy as jnp
from jax import lax
from jax.experimental import pallas as pl
from jax.experimental.pallas import tpu as pltpu

F32 = jnp.float32
BF16 = jnp.bfloat16
I32 = jnp.int32

D_MODEL = 1024
CONV_K = 31
CONV_HALO = 16
GLA_HEADS = 4
D_K = 512
D_V = 1024
DK_HEAD = D_K // GLA_HEADS
DV_HEAD = D_V // GLA_HEADS
GATE_RANK = 16
GATE_TEMP = 16.0
XA_HEADS = 4
XA_HEAD = D_MODEL // XA_HEADS
N_EXPERTS = 16
D_FF = 2048
CAPACITY_FACTOR = 2
EPS = 1e-6

LANE = 128
SUBLANE = 8
BF16_ROWS = 16
VMEM_LIMIT = 56 * 1024 * 1024

U_A, U_B, U_Q, U_K, U_V, U_G, U_XQ, U_MG = 0, 1024, 2048, 2560, 3072, 4096, 5120, 6144
U_COLS = 9216
LR_COLS = LANE

GLA_CHUNK = 128
GLA_CHUNKS_PER_STEP = 4
COMBINE_MAX_CHUNKS = LANE // BF16_ROWS + 1
COMBINE_KT = 256


def _cparams(sem):
    return pltpu.CompilerParams(dimension_semantics=sem, vmem_limit_bytes=VMEM_LIMIT)


def _split_bf16(a):
    hi = a.astype(BF16)
    lo = (a - hi.astype(F32)).astype(BF16)
    return hi, lo


def _dot(a, b, dims=None):
    if dims is None:
        return jnp.dot(a, b, preferred_element_type=F32)
    return lax.dot_general(a, b, (dims, ((), ())), preferred_element_type=F32)


def _dot3(a, b, dims=None):
    ah, al = _split_bf16(a)
    bh, bl = _split_bf16(b)
    return _dot(ah, bh, dims) + (_dot(al, bh, dims) + _dot(ah, bl, dims))


def _rms(x, g):
    return x * lax.rsqrt(jnp.mean(x * x, axis=-1, keepdims=True) + EPS) * g


def _normmm_kernel(x_ref, g_ref, w_ref, *rest, with_lr):
    if with_lr:
        wlr_ref, u_ref, lr_ref, hn_ref = rest
    else:
        u_ref, hn_ref = rest

    @pl.when(pl.program_id(1) == 0)
    def _():
        hn = _rms(x_ref[...], g_ref[...]).astype(BF16)
        hn_ref[...] = hn
        if with_lr:
            lr_ref[...] = _dot(hn, wlr_ref[...])

    u_ref[...] = _dot(hn_ref[...], w_ref[...]).astype(u_ref.dtype)


def _norm_matmul(x, g, w, w_lr=None, *, tm, tn, name):
    n, d = x.shape
    cols = w.shape[1]
    with_lr = w_lr is not None
    in_specs = [pl.BlockSpec((tm, d), lambda i, j: (i, 0)),
                pl.BlockSpec((1, d), lambda i, j: (0, 0)),
                pl.BlockSpec((d, tn), lambda i, j: (0, j))]
    out_shape = [jax.ShapeDtypeStruct((n, cols), BF16)]
    out_specs = [pl.BlockSpec((tm, tn), lambda i, j: (i, j))]
    args = [x, g, w]
    if with_lr:
        in_specs.append(pl.BlockSpec((d, LR_COLS), lambda i, j: (0, 0)))
        out_shape.append(jax.ShapeDtypeStruct((n, LR_COLS), F32))
        out_specs.append(pl.BlockSpec((tm, LR_COLS), lambda i, j: (i, 0)))
        args.append(w_lr)
    out = pl.pallas_call(
        functools.partial(_normmm_kernel, with_lr=with_lr),
        grid=(n // tm, cols // tn),
        in_specs=in_specs, out_specs=out_specs, out_shape=out_shape,
        scratch_shapes=[pltpu.VMEM((tm, d), BF16)],
        compiler_params=_cparams(("parallel", "arbitrary")),
        name=name,
    )(*args)
    return out if with_lr else out[0]


def _conv_kernel(ac_ref, bc_ref, ap_ref, bp_ref, an_ref, bn_ref, w_ref, cb_ref, g_ref, be_ref,
                 o_ref, hs_ref, co_ref, *, T):
    i = pl.program_id(1)
    n_t = pl.num_programs(1)
    H = CONV_HALO
    rows = T + 2 * H

    def glu(a_ref, b_ref):
        return a_ref[...].astype(F32) * jax.nn.sigmoid(b_ref[...].astype(F32))

    hs_ref[0, 0:H, :] = jnp.where(i > 0, glu(ap_ref, bp_ref), 0.0)
    hs_ref[0, H:H + T, :] = glu(ac_ref, bc_ref)
    hs_ref[0, H + T:rows, :] = jnp.where(i < n_t - 1, glu(an_ref, bn_ref), 0.0)
    for s in range(1, SUBLANE):
        hs_ref[s, 0:rows - SUBLANE, :] = hs_ref[0, s:s + rows - SUBLANE, :]

    RG = 64
    first = H - CONV_K // 2
    for c in range(D_MODEL // LANE):
        cl = slice(c * LANE, (c + 1) * LANE)
        taps = [w_ref[k * SUBLANE:(k + 1) * SUBLANE, cl] for k in range(CONV_K)]
        bias = jnp.broadcast_to(cb_ref[:, cl], (SUBLANE, LANE))

        def body(g, carry, cl=cl, taps=taps, bias=bias):
            r0 = pl.multiple_of(g * RG, RG)
            for v in range(RG // SUBLANE):
                acc = bias
                for k in range(CONV_K):
                    a, s = divmod(first + k, SUBLANE)
                    acc = acc + hs_ref[s, pl.ds(r0 + (a + v) * SUBLANE, SUBLANE), cl] * taps[k]
                co_ref[pl.ds(r0 + v * SUBLANE, SUBLANE), cl] = acc
            return carry

        lax.fori_loop(0, T // RG, body, 0)

    LG = 64
    for r0 in range(0, T, LG):
        hf = co_ref[r0:r0 + LG, :]
        mu = jnp.mean(hf, axis=-1, keepdims=True)
        cen = hf - mu
        var = jnp.mean(cen * cen, axis=-1, keepdims=True)
        y = cen * lax.rsqrt(var + EPS) * g_ref[...] + be_ref[...]
        o_ref[r0:r0 + LG, :] = (y * jax.nn.sigmoid(y)).astype(o_ref.dtype)


def _conv_branch(u, dw_w8, dw_b, ln_g, ln_b, *, B, S, T):
    n = B * S
    n_t = S // T
    hb = T // CONV_HALO
    last_hb = n // CONV_HALO - 1
    D = D_MODEL

    def cur(col):
        return pl.BlockSpec((T, D), lambda b, i: (b * n_t + i, col))

    def prev(col):
        return pl.BlockSpec((CONV_HALO, D), lambda b, i: (jnp.maximum((b * n_t + i) * hb - 1, 0), col))

    def nxt(col):
        return pl.BlockSpec((CONV_HALO, D), lambda b, i: (jnp.minimum((b * n_t + i + 1) * hb, last_hb), col))

    def const(shape):
        return pl.BlockSpec(shape, lambda b, i: (0, 0))

    ca, cb = U_A // D, U_B // D
    return pl.pallas_call(
        functools.partial(_conv_kernel, T=T),
        grid=(B, n_t),
        in_specs=[cur(ca), cur(cb), prev(ca), prev(cb), nxt(ca), nxt(cb),
                  const((CONV_K * SUBLANE, D)), const((1, D)), const((1, D)), const((1, D))],
        out_specs=pl.BlockSpec((T, D), lambda b, i: (b * n_t + i, 0)),
        out_shape=jax.ShapeDtypeStruct((n, D), BF16),
        scratch_shapes=[pltpu.VMEM((SUBLANE, T + 2 * CONV_HALO, D), F32), pltpu.VMEM((T, D), F32)],
        compiler_params=_cparams(("parallel", "parallel")),
        name="conv",
    )(u, u, u, u, u, u, dw_w8, dw_b, ln_g, ln_b)


def _gla_kernel(qf_ref, kf_ref, vf_ref, lf_ref, qb_ref, kb_ref, vb_ref, lb_ref, w2_ref, bg_ref,
                of_ref, ob_ref, st_ref, *, C, G):
    @pl.when(pl.program_id(1) == 0)
    def _():
        st_ref[...] = jnp.zeros_like(st_ref)

    row = lax.broadcasted_iota(I32, (C, C), 0)
    col = lax.broadcasted_iota(I32, (C, C), 1)
    scale = DK_HEAD ** -0.5

    def direction(d, q_ref, k_ref, v_ref, l_ref, o_ref):
        fwd = d == 0
        z = _dot3(l_ref[...], w2_ref[d]) + bg_ref[d]
        la = (jnp.minimum(z, 0.0) - jnp.log1p(jnp.exp(-jnp.abs(z)))) * (1.0 / GATE_TEMP)
        la_hi, la_lo = _split_bf16(la)
        tri = ((row >= col) if fwd else (row <= col)).astype(BF16)
        mask = (row >= col) if fwd else (row < col)
        states = [st_ref[d, h] for h in range(GLA_HEADS)]
        for g in (range(G) if fwd else range(G - 1, -1, -1)):
            rows = slice(g * C, (g + 1) * C)
            b = _dot(tri, la_hi[rows]) + _dot(tri, la_lo[rows])
            b_far = b[C - 1:C, :] if fwd else b[0:1, :]
            q = q_ref[rows, :].astype(F32) * scale
            k = k_ref[rows, :].astype(F32)
            qs = (q * jnp.exp(b)).astype(BF16)
            ks = (k * jnp.exp(-b)).astype(BF16)
            kd = (k * jnp.exp(b_far - b)).astype(BF16)
            carry_decay = jnp.exp(b_far)
            for h in range(GLA_HEADS):
                kl = slice(h * DK_HEAD, (h + 1) * DK_HEAD)
                vl = slice(h * DV_HEAD, (h + 1) * DV_HEAD)
                attn = _dot(qs[:, kl], ks[:, kl], ((1,), (1,)))
                attn = jnp.where(mask, attn, 0.0).astype(BF16)
                v_h = v_ref[rows, vl]
                o_ref[rows, vl] = _dot(attn, v_h) + _dot(qs[:, kl], states[h].astype(BF16), ((1,), (1,)))
                states[h] = states[h] * carry_decay[:, kl] + _dot(v_h, kd[:, kl], ((0,), (0,)))
        for h in range(GLA_HEADS):
            st_ref[d, h] = states[h]

    direction(0, qf_ref, kf_ref, vf_ref, lf_ref, of_ref)
    direction(1, qb_ref, kb_ref, vb_ref, lb_ref, ob_ref)


def _gla_branch(u, lr, w2p, bg, *, B, S):
    n = B * S
    G = GLA_CHUNKS_PER_STEP
    C = GLA_CHUNK * G
    n_c = S // C

    def fwd_map(col):
        return lambda b, c: (b * n_c + c, col)

    def bwd_map(col):
        return lambda b, c: (b * n_c + (n_c - 1 - c), col)

    def specs(m):
        return [pl.BlockSpec((C, D_K), m(U_Q // D_K)), pl.BlockSpec((C, D_K), m(U_K // D_K)),
                pl.BlockSpec((C, D_V), m(U_V // D_V)), pl.BlockSpec((C, LR_COLS), m(0))]

    return pl.pallas_call(
        functools.partial(_gla_kernel, C=GLA_CHUNK, G=G),
        grid=(B, n_c),
        in_specs=specs(fwd_map) + specs(bwd_map) + [
            pl.BlockSpec((2, LR_COLS, D_K), lambda b, c: (0, 0, 0)),
            pl.BlockSpec((2, 1, D_K), lambda b, c: (0, 0, 0))],
        out_specs=[pl.BlockSpec((C, D_V), fwd_map(0)), pl.BlockSpec((C, D_V), bwd_map(0))],
        out_shape=[jax.ShapeDtypeStruct((n, D_V), F32), jax.ShapeDtypeStruct((n, D_V), F32)],
        scratch_shapes=[pltpu.VMEM((2, GLA_HEADS, DV_HEAD, DK_HEAD), F32)],
        compiler_params=_cparams(("parallel", "arbitrary")),
        name="gla",
    )(u, u, u, lr, u, u, u, lr, w2p, bg)


def _xattn_kernel(q_ref, k_ref, v_ref, o_ref):
    scale = XA_HEAD ** -0.5
    for h in range(XA_HEADS):
        hl = slice(h * XA_HEAD, (h + 1) * XA_HEAD)
        s = _dot(q_ref[:, hl], k_ref[:, hl], ((1,), (1,))) * scale
        p = jnp.exp(s - jnp.max(s, axis=-1, keepdims=True))
        o = _dot(p.astype(BF16), v_ref[:, hl]) / jnp.sum(p, axis=-1, keepdims=True)
        o_ref[:, hl] = o.astype(o_ref.dtype)


def _xattn(u, kv, *, B, S, M, T):
    n_t = S // T
    D = D_MODEL
    return pl.pallas_call(
        _xattn_kernel,
        grid=(B, n_t),
        in_specs=[pl.BlockSpec((T, D), lambda b, i: (b * n_t + i, U_XQ // D)),
                  pl.BlockSpec((M, D), lambda b, i: (b, 0)),
                  pl.BlockSpec((M, D), lambda b, i: (b, 1))],
        out_specs=pl.BlockSpec((T, D), lambda b, i: (b * n_t + i, 0)),
        out_shape=jax.ShapeDtypeStruct((B * S, D), BF16),
        compiler_params=_cparams(("parallel", "parallel")),
        name="xattn",
    )(u, kv, kv)


def _merge_kernel(ha_ref, of_ref, ob_ref, go_ref, hc_ref, mga_ref, mgb_ref, mgc_ref, x_ref,
                  pw_ref, glao_ref, xao_ref, wout_ref, bm_ref, gng_ref, fng_ref, wr_ref,
                  xo_ref, aff_ref):
    o = of_ref[...] + ob_ref[...]
    parts = []
    for h in range(GLA_HEADS):
        oh = o[:, h * DV_HEAD:(h + 1) * DV_HEAD]
        parts.append(oh * lax.rsqrt(jnp.mean(oh * oh, axis=-1, keepdims=True) + EPS))
    on = jnp.concatenate(parts, axis=-1) * gng_ref[...]
    go = go_ref[...].astype(F32)
    hb = (on * (go * jax.nn.sigmoid(go))).astype(BF16)

    bm = bm_ref[...]
    D = D_MODEL

    def gate(mg_ref, j):
        return jax.nn.sigmoid(mg_ref[...].astype(F32) + bm[:, j * D:(j + 1) * D])

    merged = gate(mga_ref, 0) * _dot(ha_ref[...], pw_ref[...])
    merged = merged + gate(mgb_ref, 1) * _dot(hb, glao_ref[...])
    merged = merged + gate(mgc_ref, 2) * _dot(hc_ref[...], xao_ref[...])
    x_new = x_ref[...] + _dot(merged.astype(BF16), wout_ref[...])
    xo_ref[...] = x_new

    h2 = _rms(x_new, fng_ref[...])
    logits = _dot3(wr_ref[...], h2, ((1,), (1,)))
    e = jnp.exp(logits - jnp.max(logits, axis=0, keepdims=True))
    aff_ref[...] = e / jnp.sum(e, axis=0, keepdims=True)


def _merge(ha, o_f, o_b, u, hc, x, pw, glao, xao, wout, bm, gng, fng, wr_t, *, T):
    n, D = x.shape

    def tile(col=0):
        return pl.BlockSpec((T, D), lambda i: (i, col))

    def const(shape):
        return pl.BlockSpec(shape, lambda i: (0, 0), pipeline_mode=pl.Buffered(1))

    mg0 = U_MG // D
    return pl.pallas_call(
        _merge_kernel,
        grid=(n // T,),
        in_specs=[tile(), tile(), tile(), tile(U_G // D), tile(), tile(mg0), tile(mg0 + 1), tile(mg0 + 2),
                  tile(), const((D, D)), const((D, D)), const((D, D)), const((D, D)),
                  const((1, 3 * D)), const((1, D)), const((1, D)), const((N_EXPERTS, D))],
        out_specs=[tile(), pl.BlockSpec((N_EXPERTS, T), lambda i: (0, i))],
        out_shape=[jax.ShapeDtypeStruct((n, D), F32), jax.ShapeDtypeStruct((N_EXPERTS, n), F32)],
        compiler_params=_cparams(("parallel",)),
        name="merge",
    )(ha, o_f, o_b, u, hc, u, u, u, x, pw, glao, xao, wout, bm, gng, fng, wr_t)


def _route_kernel(a_ref, slot_ref, rowoff_ref, idx_ref, gate_ref, incl_ref, *, R, cap):
    E = N_EXPERTS
    bits = pltpu.bitcast(a_ref[...], I32)

    thr = jnp.zeros((E, 1, 1), I32)
    for bit in range(30, -1, -1):
        cand = thr | (1 << bit)
        cnt = jnp.sum((bits >= cand).astype(I32), axis=(1, 2), keepdims=True)
        thr = jnp.where(cnt >= cap, cand, thr)
    gt = bits > thr
    eq = bits == thr
    need = (cap - jnp.sum(gt.astype(I32), axis=(1, 2), keepdims=True)).astype(F32)

    li0 = lax.broadcasted_iota(I32, (LANE, LANE), 0)
    li1 = lax.broadcasted_iota(I32, (LANE, LANE), 1)
    upper = (li0 <= li1).astype(BF16)
    ri0 = lax.broadcasted_iota(I32, (R, R), 0)
    ri1 = lax.broadcasted_iota(I32, (R, R), 1)
    lower_strict = (ri0 > ri1).astype(BF16)
    upper_r = (ri0 <= ri1).astype(BF16)

    def prefix(m):
        incl = _dot(m.astype(BF16).reshape(E * R, LANE), upper).reshape(E, R, LANE)
        tot = jnp.broadcast_to(incl[:, :, LANE - 1:LANE], (E, R, LANE)).astype(BF16)
        off = jnp.stack([_dot(lower_strict, tot[e]) for e in range(E)])
        return incl, off

    gtf, eqf = gt.astype(F32), eq.astype(F32)
    incl_gt, off_gt = prefix(gtf)
    incl_eq, off_eq = prefix(eqf)
    excl_eq = incl_eq - eqf + off_eq
    sel = gt | (eq & (excl_eq < need))
    self_ = sel.astype(F32)
    rowoff = off_gt + jnp.minimum(off_eq, need)
    pos = (incl_gt - gtf + off_gt) + jnp.minimum(excl_eq, need)
    slot_ref[...] = jnp.where(sel, pos, -1.0).astype(I32)
    rowoff_ref[...] = rowoff.astype(I32)
    incl_ref[...] = pos - rowoff + self_

    s_col = lax.broadcasted_iota(I32, (cap, R), 0).astype(F32)
    s_col_l = lax.broadcasted_iota(I32, (cap, LANE), 0).astype(F32)
    ones8 = jnp.ones((SUBLANE, LANE), BF16)
    r_lane = lax.broadcasted_iota(I32, (SUBLANE, R), 1).astype(BF16)

    def idx_body(e, carry):
        incl_e = incl_ref[e]
        sel_e = (slot_ref[e] >= 0).astype(BF16)
        off_e = rowoff_ref[e].astype(F32)
        tot_lane = _dot(ones8, sel_e, ((1,), (1,)))
        cum_lane = _dot(tot_lane.astype(BF16), upper_r)
        cum1 = cum_lane[0:1, :]
        start1 = cum1 - tot_lane[0:1, :]
        onehot = ((start1 <= s_col) & (s_col < cum1)).astype(BF16)
        off_hi = jnp.floor(off_e * (1.0 / LANE))
        off_lo = off_e - off_hi * LANE
        off_s = _dot(onehot, off_hi.astype(BF16)) * LANE + _dot(onehot, off_lo.astype(BF16))
        incl_row = _dot(onehot, incl_e.astype(BF16))
        rank = s_col_l - off_s
        before = (incl_row <= rank).astype(BF16)
        col_lane = _dot(ones8, before, ((1,), (1,)))
        row_lane = _dot(r_lane, onehot, ((1,), (1,)))
        idx_ref[pl.ds(e, 1), :] = (row_lane[0:1, :] * LANE + col_lane[0:1, :]).astype(I32)
        a_e = a_ref[e]
        a_hi = a_e.astype(BF16)
        a_mid = (a_e - a_hi.astype(F32)).astype(BF16)
        a_lo = (a_e - a_hi.astype(F32) - a_mid.astype(F32)).astype(BF16)
        aff_row = _dot(onehot, a_hi) + (_dot(onehot, a_mid) + _dot(onehot, a_lo))
        at_col = (incl_row == rank + 1.0) & (_dot(onehot, sel_e) > 0.5)
        gate = jnp.sum(jnp.where(at_col, aff_row, 0.0), axis=-1, keepdims=True)
        gate_ref[e] = jnp.broadcast_to(gate, (cap, LANE))
        return carry

    lax.fori_loop(0, E, idx_body, 0)


def _route(aff_t, *, n):
    R = n // LANE
    cap = CAPACITY_FACTOR * n // N_EXPERTS
    E = N_EXPERTS
    a3 = aff_t.reshape(E, R, LANE)
    slot, rowoff, idx, gate = pl.pallas_call(
        functools.partial(_route_kernel, R=R, cap=cap),
        out_shape=[jax.ShapeDtypeStruct((E, R, LANE), I32), jax.ShapeDtypeStruct((E, R, LANE), I32),
                   jax.ShapeDtypeStruct((E, cap), I32), jax.ShapeDtypeStruct((E, cap, LANE), F32)],
        scratch_shapes=[pltpu.VMEM((E, R, LANE), F32)],
        compiler_params=pltpu.CompilerParams(vmem_limit_bytes=VMEM_LIMIT),
        name="route",
    )(a3)
    return slot, rowoff[:, :, 0], idx, gate


def _ffn_kernel(idx_ref, x_ref, ng_ref, gate_ref, wg_ref, wu_ref, wd_ref, o_ref, xbuf_ref, hn_ref, acc_ref,
                sem_ref, *, tm):
    n_m = pl.num_programs(1)
    n_f = pl.num_programs(2)
    f = pl.program_id(2)
    tile = pl.program_id(0) * n_m + pl.program_id(1)
    n_tiles = pl.num_programs(0) * n_m
    per_step = tm // n_f

    def row_copy(t, i, buf):
        src = idx_ref[t * tm + i]
        return pltpu.make_async_copy(x_ref.at[pl.ds(src, 1), :], xbuf_ref.at[buf, pl.ds(i, 1), :], sem_ref.at[buf])

    @pl.when((tile == 0) & (f == 0))
    def _():
        lax.fori_loop(0, tm, lambda i, c: (row_copy(0, i, 0).start(), c)[1], 0)

    def wait_rows(buf):
        pltpu.make_async_copy(x_ref.at[pl.ds(0, tm), :], xbuf_ref.at[buf], sem_ref.at[buf]).wait()

    @pl.when(f == 0)
    def _():
        wait_rows(tile % 2)
        hn_ref[...] = _rms(xbuf_ref[tile % 2], ng_ref[...]).astype(BF16)

    nxt = jnp.minimum(tile + 1, n_tiles - 1)
    for i in range(per_step):
        row_copy(nxt, f * per_step + i, (tile + 1) % 2).start()

    x = hn_ref[...]
    g = _dot(x, wg_ref[0, 0].astype(BF16))
    up = _dot(x, wu_ref[0, 0].astype(BF16))
    hid = (g * jax.nn.sigmoid(g) * up).astype(BF16)
    part = _dot(hid, wd_ref[0, 0].astype(BF16))

    @pl.when(f == 0)
    def _():
        acc_ref[...] = part

    @pl.when(f > 0)
    def _():
        acc_ref[...] += part

    @pl.when(f == pl.num_programs(2) - 1)
    def _():
        gate = gate_ref[0]
        for c in range(D_MODEL // LANE):
            cl = slice(c * LANE, (c + 1) * LANE)
            o_ref[0, :, cl] = (acc_ref[:, cl] * gate).astype(o_ref.dtype)

        @pl.when(tile == n_tiles - 1)
        def _():
            wait_rows((tile + 1) % 2)


def _expert_ffn(x, idx, ng, gate, wg, wu, wd, layer, *, tm, tf):
    D = x.shape[1]
    E, cap, _ = gate.shape
    F = wg.shape[3]
    grid_spec = pltpu.PrefetchScalarGridSpec(
        num_scalar_prefetch=1,
        grid=(E, cap // tm, F // tf),
        in_specs=[pl.BlockSpec(memory_space=pl.ANY),
                  pl.BlockSpec((1, D), lambda e, m, f, *_: (0, 0)),
                  pl.BlockSpec((1, tm, LANE), lambda e, m, f, *_: (e, m, 0)),
                  pl.BlockSpec((1, 1, D, tf), lambda e, m, f, *_: (layer, e, 0, f)),
                  pl.BlockSpec((1, 1, D, tf), lambda e, m, f, *_: (layer, e, 0, f)),
                  pl.BlockSpec((1, 1, tf, D), lambda e, m, f, *_: (layer, e, f, 0))],
        out_specs=pl.BlockSpec((1, tm, D), lambda e, m, f, *_: (e, m, 0)),
        scratch_shapes=[pltpu.VMEM((2, tm, D), F32), pltpu.VMEM((tm, D), BF16), pltpu.VMEM((tm, D), F32),
                        pltpu.SemaphoreType.DMA((2,))],
    )
    return pl.pallas_call(
        functools.partial(_ffn_kernel, tm=tm),
        grid_spec=grid_spec,
        out_shape=jax.ShapeDtypeStruct((E, cap, D), BF16),
        compiler_params=_cparams(("arbitrary", "arbitrary", "arbitrary")),
        name="ffn",
    )(idx, x, ng, gate, wg, wu, wd)


def _combine_plan(slot, rowoff, *, cap):
    E, R = rowoff.shape
    CH = BF16_ROWS
    nxt = jnp.concatenate([rowoff[:, 1:], jnp.full((E, 1), cap, I32)], axis=1)
    cnt = nxt - rowoff
    c0 = rowoff // CH
    nch = jnp.where(cnt > 0, (nxt + CH - 1) // CH - c0, 0)
    cum = jnp.cumsum(nch, axis=0)
    total = cum[-1]
    kc = E * COMBINE_MAX_CHUNKS
    p = jnp.arange(kc, dtype=I32)
    e_of = jnp.minimum(jnp.sum((cum[:, :, None] <= p[None, None, :]).astype(I32), axis=0), E - 1)
    e_is = e_of[None, :, :] == jnp.arange(E, dtype=I32)[:, None, None]
    pick = lambda a: jnp.sum(jnp.where(e_is, a[:, :, None], 0), axis=0)
    in_expert = pick(c0) + (p[None, :] - pick(cum - nch))
    valid = p[None, :] < total[:, None]
    chunk = jnp.where(valid, e_of * (cap // CH) + in_expert, 0)
    row_expert = jnp.broadcast_to(jnp.where(valid, e_of, -1)[:, :, None], (R, kc, CH))
    row_slot = jnp.where(valid[:, :, None], in_expert[:, :, None] * CH + jnp.arange(CH, dtype=I32), -2).astype(F32)
    return (chunk.reshape(-1), total, row_expert.reshape(R, 1, kc * CH), row_slot.reshape(R, 1, kc * CH),
            slot.reshape(E, -1).T)


def _combine_kernel(chunks_ref, total_ref, x_ref, slot_ref, rowe_ref, rows_ref, ye_ref, fg_ref, o_ref,
                    buf_ref, sem_ref, *, kc, final_norm):
    r = pl.program_id(0)
    n_r = pl.num_programs(0)
    CH = BF16_ROWS
    KT = COMBINE_KT
    KTC = KT // CH

    def start_all(rr, b):
        def issue(i, c):
            pltpu.make_async_copy(ye_ref.at[chunks_ref[rr * kc + i]], buf_ref.at[b, i], sem_ref.at[b]).start()
            return c

        lax.fori_loop(0, total_ref[rr], issue, 0)

    @pl.when(r == 0)
    def _():
        buf_ref[...] = jnp.zeros_like(buf_ref)
        start_all(0, 0)

    @pl.when(r + 1 < n_r)
    def _():
        start_all(r + 1, (r + 1) % 2)

    b = r % 2
    n_chunks = total_ref[r]
    for bit in range(kc.bit_length()):
        @pl.when(((n_chunks >> bit) & 1) == 1)
        def _(m=1 << bit):
            pltpu.make_async_copy(ye_ref.at[pl.ds(0, m)], buf_ref.at[b, pl.ds(0, m)], sem_ref.at[b]).wait()

    SPLIT = 64
    slot = slot_ref[...]
    slot_hi = (slot >> 6).astype(F32).astype(BF16)
    slot_lo = (slot & (SPLIT - 1)).astype(F32).astype(BF16)
    expert_iota = lax.broadcasted_iota(I32, (N_EXPERTS, KT), 0)
    o_ref[...] = x_ref[...]

    def k_body(kt, carry):
        k0 = pl.multiple_of(kt * KT, KT)
        of_expert = (expert_iota == rowe_ref[0, :, pl.ds(k0, KT)]).astype(F32).astype(BF16)
        wanted = _dot(slot_hi, of_expert) * SPLIT + _dot(slot_lo, of_expert)
        onehot = jnp.where(wanted == rows_ref[0, :, pl.ds(k0, KT)], 1.0, 0.0).astype(BF16)
        rows = buf_ref[b, pl.ds(pl.multiple_of(kt * KTC, KTC), KTC)].reshape(KT, D_MODEL)
        o_ref[...] += _dot(onehot, rows)
        return carry

    lax.fori_loop(0, (total_ref[r] * CH + KT - 1) // KT, k_body, 0)
    if final_norm:
        o_ref[...] = _rms(o_ref[...], fg_ref[...])


def _combine(x, ye, plan, fg, *, final_norm):
    n, D = x.shape
    E, cap, _ = ye.shape
    R = n // LANE
    chunks, total, row_expert, row_slot, slot_tok = plan
    kc = E * COMBINE_MAX_CHUNKS
    ye_chunks = ye.reshape(E * cap // BF16_ROWS, BF16_ROWS, D)
    row_spec = pl.BlockSpec((1, 1, kc * BF16_ROWS), lambda r, *_: (r, 0, 0))
    grid_spec = pltpu.PrefetchScalarGridSpec(
        num_scalar_prefetch=2,
        grid=(R,),
        in_specs=[pl.BlockSpec((LANE, D), lambda r, *_: (r, 0)),
                  pl.BlockSpec((LANE, E), lambda r, *_: (r, 0)),
                  row_spec, row_spec,
                  pl.BlockSpec(memory_space=pl.ANY),
                  pl.BlockSpec((1, D), lambda r, *_: (0, 0))],
        out_specs=pl.BlockSpec((LANE, D), lambda r, *_: (r, 0)),
        scratch_shapes=[pltpu.VMEM((2, kc, BF16_ROWS, D), BF16), pltpu.SemaphoreType.DMA((2,))],
    )
    return pl.pallas_call(
        functools.partial(_combine_kernel, kc=kc, final_norm=final_norm),
        grid_spec=grid_spec,
        out_shape=jax.ShapeDtypeStruct((n, D), F32),
        compiler_params=_cparams(("arbitrary",)),
        name="combine",
    )(chunks, total, x, slot_tok, row_expert, row_slot, ye_chunks, fg)


def _run_trunk(x, mem, p):
    B, S, D = x.shape
    M = mem.shape[1]
    n = B * S
    cap = CAPACITY_FACTOR * n // N_EXPERTS
    x = x.reshape(n, D)
    mem2 = mem.reshape(B * M, D)
    depth = p["w_main"].shape[0]
    conv_t = min(256, S)
    xa_t = min(512, S)
    for l in range(depth):
        u, lr = _norm_matmul(x, p["norm_mix_g"][l], p["w_main"][l], p["w_lr"][l],
                             tm=min(1024, n), tn=1024, name="inproj")
        ha = _conv_branch(u, p["dw_w8"][l], p["conv_dw_b"][l], p["conv_ln_g"][l], p["conv_ln_b"][l],
                          B=B, S=S, T=conv_t)
        o_f, o_b = _gla_branch(u, lr, p["w2p"][l], p["gla_gate_b"][l], B=B, S=S)
        kv = _norm_matmul(mem2, p["mem_norm_g"][l], p["w_mem_kv"][l], tm=M, tn=1024, name="memkv")
        hc = _xattn(u, kv, B=B, S=S, M=M, T=xa_t)
        x, aff_t = _merge(ha, o_f, o_b, u, hc, x, p["conv_pw_w"][l], p["gla_o"][l], p["xa_o"][l],
                          p["w_out"][l], p["b_merge"][l], p["gla_norm_g"][l], p["norm_ffn_g"][l],
                          p["w_router_t"][l], T=min(512, n))
        slot, rowoff, idx, gate = _route(aff_t, n=n)
        ye = _expert_ffn(x, idx.reshape(-1), p["norm_ffn_g"][l], gate, p["w_gate_e"], p["w_up_e"],
                         p["w_down_e"], l, tm=min(1024, cap), tf=512)
        x = _combine(x, ye, _combine_plan(slot, rowoff, cap=cap), p["final_norm_g"],
                     final_norm=(l == depth - 1))
    return x.reshape(B, S, D)


def _prepare(norm_mix_g, w_in, b_merge, conv_dw_w, conv_dw_b, conv_ln_g, conv_ln_b, conv_pw_w,
             gla_gate_w2, gla_gate_b, gla_norm_g, gla_o, mem_norm_g, w_mem_kv, xa_o, w_out, norm_ffn_g,
             w_router, w_gate_e, w_up_e, w_down_e, final_norm_g):
    L = w_in.shape[0]
    lr0 = U_XQ
    lr1 = lr0 + 2 * GATE_RANK
    w_main = jnp.concatenate([w_in[:, :, :lr0], w_in[:, :, lr1:]], axis=-1).astype(BF16)
    w_lr = jnp.pad(w_in[:, :, lr0:lr1], ((0, 0), (0, 0), (0, LR_COLS - 2 * GATE_RANK))).astype(BF16)
    w2p = jnp.zeros((L, 2, LR_COLS, D_K), F32)
    w2p = w2p.at[:, 0, 0:GATE_RANK].set(gla_gate_w2[:, 0])
    w2p = w2p.at[:, 1, GATE_RANK:2 * GATE_RANK].set(gla_gate_w2[:, 1])
    row = lambda a: a[:, None, :]
    return dict(
        norm_mix_g=row(norm_mix_g), w_main=w_main, w_lr=w_lr, b_merge=row(b_merge),
        dw_w8=jnp.repeat(conv_dw_w, SUBLANE, axis=1), conv_dw_b=row(conv_dw_b),
        conv_ln_g=row(conv_ln_g), conv_ln_b=row(conv_ln_b), conv_pw_w=conv_pw_w.astype(BF16),
        w2p=w2p, gla_gate_b=gla_gate_b[:, :, None, :], gla_norm_g=row(gla_norm_g), gla_o=gla_o.astype(BF16),
        mem_norm_g=row(mem_norm_g), w_mem_kv=w_mem_kv.astype(BF16), xa_o=xa_o.astype(BF16),
        w_out=w_out.astype(BF16), norm_ffn_g=row(norm_ffn_g), w_router_t=jnp.swapaxes(w_router, 1, 2),
        w_gate_e=w_gate_e, w_up_e=w_up_e, w_down_e=w_down_e,
        final_norm_g=final_norm_g[None, :],
    )


def kernel(x_prompt, x_sample, mem_prompt, mem_sample, norm_mix_g, w_in, b_merge, conv_dw_w, conv_dw_b, conv_ln_g, conv_ln_b, conv_pw_w, gla_gate_w2, gla_gate_b, gla_norm_g, gla_o, mem_norm_g, w_mem_kv, xa_o, w_out, norm_ffn_g, w_router, w_gate_e, w_up_e, w_down_e, final_norm_g):
    p = _prepare(norm_mix_g, w_in, b_merge, conv_dw_w, conv_dw_b, conv_ln_g, conv_ln_b, conv_pw_w,
                 gla_gate_w2, gla_gate_b, gla_norm_g, gla_o, mem_norm_g, w_mem_kv, xa_o, w_out, norm_ffn_g,
                 w_router, w_gate_e, w_up_e, w_down_e, final_norm_g)
    return (_run_trunk(x_prompt, mem_prompt, p), _run_trunk(x_sample, mem_sample, p))
```

```python
import functools

import jax
import jax.numpy as jnp
from jax import lax
from jax.experimental import pallas as pl
from jax.experimental.pallas import tpu as pltpu

F32 = jnp.float32
BF16 = jnp.bfloat16
I32 = jnp.int32

D_MODEL = 1024
CONV_K = 31
CONV_HALO = 16
GLA_HEADS = 4
D_K = 512
D_V = 1024
DK_HEAD = D_K // GLA_HEADS
DV_HEAD = D_V // GLA_HEADS
GATE_RANK = 16
GATE_TEMP = 16.0
XA_HEADS = 4
XA_HEAD = D_MODEL // XA_HEADS
N_EXPERTS = 16
D_FF = 2048
CAPACITY_FACTOR = 2
EPS = 1e-6

LANE = 128
SUBLANE = 8
BF16_ROWS = 16
VMEM_LIMIT = 56 * 1024 * 1024

U_A, U_B, U_Q, U_K, U_V, U_G, U_XQ, U_MG = 0, 1024, 2048, 2560, 3072, 4096, 5120, 6144
U_COLS = 9216
LR_COLS = LANE

GLA_CHUNK = 128
GLA_CHUNKS_PER_STEP = 4
COMBINE_MAX_CHUNKS = LANE // BF16_ROWS + 1
COMBINE_KT = 256


def _cparams(sem):
    return pltpu.CompilerParams(dimension_semantics=sem, vmem_limit_bytes=VMEM_LIMIT)


def _split_bf16(a):
    hi = a.astype(BF16)
    lo = (a - hi.astype(F32)).astype(BF16)
    return hi, lo


def _dot(a, b, dims=None):
    if dims is None:
        return jnp.dot(a, b, preferred_element_type=F32)
    return lax.dot_general(a, b, (dims, ((), ())), preferred_element_type=F32)


def _dot3(a, b, dims=None):
    ah, al = _split_bf16(a)
    bh, bl = _split_bf16(b)
    return _dot(ah, bh, dims) + (_dot(al, bh, dims) + _dot(ah, bl, dims))


def _rms(x, g):
    return x * lax.rsqrt(jnp.mean(x * x, axis=-1, keepdims=True) + EPS) * g


def _normmm_kernel(x_ref, g_ref, w_ref, *rest, with_lr):
    if with_lr:
        wlr_ref, u_ref, lr_ref, hn_ref = rest
    else:
        u_ref, hn_ref = rest

    @pl.when(pl.program_id(1) == 0)
    def _():
        hn = _rms(x_ref[...], g_ref[...]).astype(BF16)
        hn_ref[...] = hn
        if with_lr:
            lr_ref[...] = _dot(hn, wlr_ref[...])

    u_ref[...] = _dot(hn_ref[...], w_ref[...]).astype(u_ref.dtype)


def _norm_matmul(x, g, w, w_lr=None, *, tm, tn, name):
    n, d = x.shape
    cols = w.shape[1]
    with_lr = w_lr is not None
    in_specs = [pl.BlockSpec((tm, d), lambda i, j: (i, 0)),
                pl.BlockSpec((1, d), lambda i, j: (0, 0)),
                pl.BlockSpec((d, tn), lambda i, j: (0, j))]
    out_shape = [jax.ShapeDtypeStruct((n, cols), BF16)]
    out_specs = [pl.BlockSpec((tm, tn), lambda i, j: (i, j))]
    args = [x, g, w]
    if with_lr:
        in_specs.append(pl.BlockSpec((d, LR_COLS), lambda i, j: (0, 0)))
        out_shape.append(jax.ShapeDtypeStruct((n, LR_COLS), F32))
        out_specs.append(pl.BlockSpec((tm, LR_COLS), lambda i, j: (i, 0)))
        args.append(w_lr)
    out = pl.pallas_call(
        functools.partial(_normmm_kernel, with_lr=with_lr),
        grid=(n // tm, cols // tn),
        in_specs=in_specs, out_specs=out_specs, out_shape=out_shape,
        scratch_shapes=[pltpu.VMEM((tm, d), BF16)],
        compiler_params=_cparams(("parallel", "arbitrary")),
        name=name,
    )(*args)
    return out if with_lr else out[0]


def _conv_kernel(ac_ref, bc_ref, ap_ref, bp_ref, an_ref, bn_ref, w_ref, cb_ref, g_ref, be_ref,
                 o_ref, hs_ref, co_ref, *, T):
    i = pl.program_id(1)
    n_t = pl.num_programs(1)
    H = CONV_HALO
    rows = T + 2 * H

    def glu(a_ref, b_ref):
        return a_ref[...].astype(F32) * jax.nn.sigmoid(b_ref[...].astype(F32))

    hs_ref[0, 0:H, :] = jnp.where(i > 0, glu(ap_ref, bp_ref), 0.0)
    hs_ref[0, H:H + T, :] = glu(ac_ref, bc_ref)
    hs_ref[0, H + T:rows, :] = jnp.where(i < n_t - 1, glu(an_ref, bn_ref), 0.0)
    for s in range(1, SUBLANE):
        hs_ref[s, 0:rows - SUBLANE, :] = hs_ref[0, s:s + rows - SUBLANE, :]

    RG = 64
    first = H - CONV_K // 2
    for c in range(D_MODEL // LANE):
        cl = slice(c * LANE, (c + 1) * LANE)
        taps = [w_ref[k * SUBLANE:(k + 1) * SUBLANE, cl] for k in range(CONV_K)]
        bias = jnp.broadcast_to(cb_ref[:, cl], (SUBLANE, LANE))

        def body(g, carry, cl=cl, taps=taps, bias=bias):
            r0 = pl.multiple_of(g * RG, RG)
            for v in range(RG // SUBLANE):
                acc = bias
                for k in range(CONV_K):
                    a, s = divmod(first + k, SUBLANE)
                    acc = acc + hs_ref[s, pl.ds(r0 + (a + v) * SUBLANE, SUBLANE), cl] * taps[k]
                co_ref[pl.ds(r0 + v * SUBLANE, SUBLANE), cl] = acc
            return carry

        lax.fori_loop(0, T // RG, body, 0)

    LG = 64
    for r0 in range(0, T, LG):
        hf = co_ref[r0:r0 + LG, :]
        mu = jnp.mean(hf, axis=-1, keepdims=True)
        cen = hf - mu
        var = jnp.mean(cen * cen, axis=-1, keepdims=True)
        y = cen * lax.rsqrt(var + EPS) * g_ref[...] + be_ref[...]
        o_ref[r0:r0 + LG, :] = (y * jax.nn.sigmoid(y)).astype(o_ref.dtype)


def _conv_branch(u, dw_w8, dw_b, ln_g, ln_b, *, B, S, T):
    n = B * S
    n_t = S // T
    hb = T // CONV_HALO
    last_hb = n // CONV_HALO - 1
    D = D_MODEL

    def cur(col):
        return pl.BlockSpec((T, D), lambda b, i: (b * n_t + i, col))

    def prev(col):
        return pl.BlockSpec((CONV_HALO, D), lambda b, i: (jnp.maximum((b * n_t + i) * hb - 1, 0), col))

    def nxt(col):
        return pl.BlockSpec((CONV_HALO, D), lambda b, i: (jnp.minimum((b * n_t + i + 1) * hb, last_hb), col))

    def const(shape):
        return pl.BlockSpec(shape, lambda b, i: (0, 0))

    ca, cb = U_A // D, U_B // D
    return pl.pallas_call(
        functools.partial(_conv_kernel, T=T),
        grid=(B, n_t),
        in_specs=[cur(ca), cur(cb), prev(ca), prev(cb), nxt(ca), nxt(cb),
                  const((CONV_K * SUBLANE, D)), const((1, D)), const((1, D)), const((1, D))],
        out_specs=pl.BlockSpec((T, D), lambda b, i: (b * n_t + i, 0)),
        out_shape=jax.ShapeDtypeStruct((n, D), BF16),
        scratch_shapes=[pltpu.VMEM((SUBLANE, T + 2 * CONV_HALO, D), F32), pltpu.VMEM((T, D), F32)],
        compiler_params=_cparams(("parallel", "parallel")),
        name="conv",
    )(u, u, u, u, u, u, dw_w8, dw_b, ln_g, ln_b)


def _gla_kernel(qf_ref, kf_ref, vf_ref, lf_ref, qb_ref, kb_ref, vb_ref, lb_ref, w2_ref, bg_ref,
                of_ref, ob_ref, st_ref, *, C, G):
    @pl.when(pl.program_id(1) == 0)
    def _():
        st_ref[...] = jnp.zeros_like(st_ref)

    row = lax.broadcasted_iota(I32, (C, C), 0)
    col = lax.broadcasted_iota(I32, (C, C), 1)
    scale = DK_HEAD ** -0.5

    def direction(d, q_ref, k_ref, v_ref, l_ref, o_ref):
        fwd = d == 0
        z = _dot3(l_ref[...], w2_ref[d]) + bg_ref[d]
        la = (jnp.minimum(z, 0.0) - jnp.log1p(jnp.exp(-jnp.abs(z)))) * (1.0 / GATE_TEMP)
        la_hi, la_lo = _split_bf16(la)
        tri = ((row >= col) if fwd else (row <= col)).astype(BF16)
        mask = (row >= col) if fwd else (row < col)
        states = [st_ref[d, h] for h in range(GLA_HEADS)]
        for g in (range(G) if fwd else range(G - 1, -1, -1)):
            rows = slice(g * C, (g + 1) * C)
            b = _dot(tri, la_hi[rows]) + _dot(tri, la_lo[rows])
            b_far = b[C - 1:C, :] if fwd else b[0:1, :]
            q = q_ref[rows, :].astype(F32) * scale
            k = k_ref[rows, :].astype(F32)
            qs = (q * jnp.exp(b)).astype(BF16)
            ks = (k * jnp.exp(-b)).astype(BF16)
            kd = (k * jnp.exp(b_far - b)).astype(BF16)
            carry_decay = jnp.exp(b_far)
            for h in range(GLA_HEADS):
                kl = slice(h * DK_HEAD, (h + 1) * DK_HEAD)
                vl = slice(h * DV_HEAD, (h + 1) * DV_HEAD)
                attn = _dot(qs[:, kl], ks[:, kl], ((1,), (1,)))
                attn = jnp.where(mask, attn, 0.0).astype(BF16)
                v_h = v_ref[rows, vl]
                o_ref[rows, vl] = _dot(attn, v_h) + _dot(qs[:, kl], states[h].astype(BF16), ((1,), (1,)))
                states[h] = states[h] * carry_decay[:, kl] + _dot(v_h, kd[:, kl], ((0,), (0,)))
        for h in range(GLA_HEADS):
            st_ref[d, h] = states[h]

    direction(0, qf_ref, kf_ref, vf_ref, lf_ref, of_ref)
    direction(1, qb_ref, kb_ref, vb_ref, lb_ref, ob_ref)


def _gla_branch(u, lr, w2p, bg, *, B, S):
    n = B * S
    G = GLA_CHUNKS_PER_STEP
    C = GLA_CHUNK * G
    n_c = S // C

    def fwd_map(col):
        return lambda b, c: (b * n_c + c, col)

    def bwd_map(col):
        return lambda b, c: (b * n_c + (n_c - 1 - c), col)

    def specs(m):
        return [pl.BlockSpec((C, D_K), m(U_Q // D_K)), pl.BlockSpec((C, D_K), m(U_K // D_K)),
                pl.BlockSpec((C, D_V), m(U_V // D_V)), pl.BlockSpec((C, LR_COLS), m(0))]

    return pl.pallas_call(
        functools.partial(_gla_kernel, C=GLA_CHUNK, G=G),
        grid=(B, n_c),
        in_specs=specs(fwd_map) + specs(bwd_map) + [
            pl.BlockSpec((2, LR_COLS, D_K), lambda b, c: (0, 0, 0)),
            pl.BlockSpec((2, 1, D_K), lambda b, c: (0, 0, 0))],
        out_specs=[pl.BlockSpec((C, D_V), fwd_map(0)), pl.BlockSpec((C, D_V), bwd_map(0))],
        out_shape=[jax.ShapeDtypeStruct((n, D_V), F32), jax.ShapeDtypeStruct((n, D_V), F32)],
        scratch_shapes=[pltpu.VMEM((2, GLA_HEADS, DV_HEAD, DK_HEAD), F32)],
        compiler_params=_cparams(("parallel", "arbitrary")),
        name="gla",
    )(u, u, u, lr, u, u, u, lr, w2p, bg)


def _xattn_kernel(q_ref, k_ref, v_ref, o_ref):
    scale = XA_HEAD ** -0.5
    for h in range(XA_HEADS):
        hl = slice(h * XA_HEAD, (h + 1) * XA_HEAD)
        s = _dot(q_ref[:, hl], k_ref[:, hl], ((1,), (1,))) * scale
        p = jnp.exp(s - jnp.max(s, axis=-1, keepdims=True))
        o = _dot(p.astype(BF16), v_ref[:, hl]) / jnp.sum(p, axis=-1, keepdims=True)
        o_ref[:, hl] = o.astype(o_ref.dtype)


def _xattn(u, kv, *, B, S, M, T):
    n_t = S // T
    D = D_MODEL
    return pl.pallas_call(
        _xattn_kernel,
        grid=(B, n_t),
        in_specs=[pl.BlockSpec((T, D), lambda b, i: (b * n_t + i, U_XQ // D)),
                  pl.BlockSpec((M, D), lambda b, i: (b, 0)),
                  pl.BlockSpec((M, D), lambda b, i: (b, 1))],
        out_specs=pl.BlockSpec((T, D), lambda b, i: (b * n_t + i, 0)),
        out_shape=jax.ShapeDtypeStruct((B * S, D), BF16),
        compiler_params=_cparams(("parallel", "parallel")),
        name="xattn",
    )(u, kv, kv)


def _merge_kernel(ha_ref, of_ref, ob_ref, go_ref, hc_ref, mga_ref, mgb_ref, mgc_ref, x_ref,
                  pw_ref, glao_ref, xao_ref, wout_ref, bm_ref, gng_ref, fng_ref, wr_ref,
                  xo_ref, aff_ref):
    o = of_ref[...] + ob_ref[...]
    parts = []
    for h in range(GLA_HEADS):
        oh = o[:, h * DV_HEAD:(h + 1) * DV_HEAD]
        parts.append(oh * lax.rsqrt(jnp.mean(oh * oh, axis=-1, keepdims=True) + EPS))
    on = jnp.concatenate(parts, axis=-1) * gng_ref[...]
    go = go_ref[...].astype(F32)
    hb = (on * (go * jax.nn.sigmoid(go))).astype(BF16)

    bm = bm_ref[...]
    D = D_MODEL

    def gate(mg_ref, j):
        return jax.nn.sigmoid(mg_ref[...].astype(F32) + bm[:, j * D:(j + 1) * D])

    merged = gate(mga_ref, 0) * _dot(ha_ref[...], pw_ref[...])
    merged = merged + gate(mgb_ref, 1) * _dot(hb, glao_ref[...])
    merged = merged + gate(mgc_ref, 2) * _dot(hc_ref[...], xao_ref[...])
    x_new = x_ref[...] + _dot(merged.astype(BF16), wout_ref[...])
    xo_ref[...] = x_new

    h2 = _rms(x_new, fng_ref[...])
    logits = _dot3(wr_ref[...], h2, ((1,), (1,)))
    e = jnp.exp(logits - jnp.max(logits, axis=0, keepdims=True))
    aff_ref[...] = e / jnp.sum(e, axis=0, keepdims=True)


def _merge(ha, o_f, o_b, u, hc, x, pw, glao, xao, wout, bm, gng, fng, wr_t, *, T):
    n, D = x.shape

    def tile(col=0):
        return pl.BlockSpec((T, D), lambda i: (i, col))

    def const(shape):
        return pl.BlockSpec(shape, lambda i: (0, 0), pipeline_mode=pl.Buffered(1))

    mg0 = U_MG // D
    return pl.pallas_call(
        _merge_kernel,
        grid=(n // T,),
        in_specs=[tile(), tile(), tile(), tile(U_G // D), tile(), tile(mg0), tile(mg0 + 1), tile(mg0 + 2),
                  tile(), const((D, D)), const((D, D)), const((D, D)), const((D, D)),
                  const((1, 3 * D)), const((1, D)), const((1, D)), const((N_EXPERTS, D))],
        out_specs=[tile(), pl.BlockSpec((N_EXPERTS, T), lambda i: (0, i))],
        out_shape=[jax.ShapeDtypeStruct((n, D), F32), jax.ShapeDtypeStruct((N_EXPERTS, n), F32)],
        compiler_params=_cparams(("parallel",)),
        name="merge",
    )(ha, o_f, o_b, u, hc, u, u, u, x, pw, glao, xao, wout, bm, gng, fng, wr_t)


def _route_kernel(a_ref, slot_ref, rowoff_ref, idx_ref, gate_ref, incl_ref, *, R, cap):
    E = N_EXPERTS
    bits = pltpu.bitcast(a_ref[...], I32)

    thr = jnp.zeros((E, 1, 1), I32)
    for bit in range(30, -1, -1):
        cand = thr | (1 << bit)
        cnt = jnp.sum((bits >= cand).astype(I32), axis=(1, 2), keepdims=True)
        thr = jnp.where(cnt >= cap, cand, thr)
    gt = bits > thr
    eq = bits == thr
    need = (cap - jnp.sum(gt.astype(I32), axis=(1, 2), keepdims=True)).astype(F32)

    li0 = lax.broadcasted_iota(I32, (LANE, LANE), 0)
    li1 = lax.broadcasted_iota(I32, (LANE, LANE), 1)
    upper = (li0 <= li1).astype(BF16)
    ri0 = lax.broadcasted_iota(I32, (R, R), 0)
    ri1 = lax.broadcasted_iota(I32, (R, R), 1)
    lower_strict = (ri0 > ri1).astype(BF16)
    upper_r = (ri0 <= ri1).astype(BF16)

    def prefix(m):
        incl = _dot(m.astype(BF16).reshape(E * R, LANE), upper).reshape(E, R, LANE)
        tot = jnp.broadcast_to(incl[:, :, LANE - 1:LANE], (E, R, LANE)).astype(BF16)
        off = jnp.stack([_dot(lower_strict, tot[e]) for e in range(E)])
        return incl, off

    gtf, eqf = gt.astype(F32), eq.astype(F32)
    incl_gt, off_gt = prefix(gtf)
    incl_eq, off_eq = prefix(eqf)
    excl_eq = incl_eq - eqf + off_eq
    sel = gt | (eq & (excl_eq < need))
    self_ = sel.astype(F32)
    rowoff = off_gt + jnp.minimum(off_eq, need)
    pos = (incl_gt - gtf + off_gt) + jnp.minimum(excl_eq, need)
    slot_ref[...] = jnp.where(sel, pos, -1.0).astype(I32)
    rowoff_ref[...] = rowoff.astype(I32)
    incl_ref[...] = pos - rowoff + self_

    s_col = lax.broadcasted_iota(I32, (cap, R), 0).astype(F32)
    s_col_l = lax.broadcasted_iota(I32, (cap, LANE), 0).astype(F32)
    ones8 = jnp.ones((SUBLANE, LANE), BF16)
    r_lane = lax.broadcasted_iota(I32, (SUBLANE, R), 1).astype(BF16)

    def idx_body(e, carry):
        incl_e = incl_ref[e]
        sel_e = (slot_ref[e] >= 0).astype(BF16)
        off_e = rowoff_ref[e].astype(F32)
        tot_lane = _dot(ones8, sel_e, ((1,), (1,)))
        cum_lane = _dot(tot_lane.astype(BF16), upper_r)
        cum1 = cum_lane[0:1, :]
        start1 = cum1 - tot_lane[0:1, :]
        onehot = ((start1 <= s_col) & (s_col < cum1)).astype(BF16)
        off_hi = jnp.floor(off_e * (1.0 / LANE))
        off_lo = off_e - off_hi * LANE
        off_s = _dot(onehot, off_hi.astype(BF16)) * LANE + _dot(onehot, off_lo.astype(BF16))
        incl_row = _dot(onehot, incl_e.astype(BF16))
        rank = s_col_l - off_s
        before = (incl_row <= rank).astype(BF16)
        col_lane = _dot(ones8, before, ((1,), (1,)))
        row_lane = _dot(r_lane, onehot, ((1,), (1,)))
        idx_ref[pl.ds(e, 1), :] = (row_lane[0:1, :] * LANE + col_lane[0:1, :]).astype(I32)
        a_e = a_ref[e]
        a_hi = a_e.astype(BF16)
        a_mid = (a_e - a_hi.astype(F32)).astype(BF16)
        a_lo = (a_e - a_hi.astype(F32) - a_mid.astype(F32)).astype(BF16)
        aff_row = _dot(onehot, a_hi) + (_dot(onehot, a_mid) + _dot(onehot, a_lo))
        at_col = (incl_row == rank + 1.0) & (_dot(onehot, sel_e) > 0.5)
        gate = jnp.sum(jnp.where(at_col, aff_row, 0.0), axis=-1, keepdims=True)
        gate_ref[e] = jnp.broadcast_to(gate, (cap, LANE))
        return carry

    lax.fori_loop(0, E, idx_body, 0)


def _route(aff_t, *, n):
    R = n // LANE
    cap = CAPACITY_FACTOR * n // N_EXPERTS
    E = N_EXPERTS
    a3 = aff_t.reshape(E, R, LANE)
    slot, rowoff, idx, gate = pl.pallas_call(
        functools.partial(_route_kernel, R=R, cap=cap),
        out_shape=[jax.ShapeDtypeStruct((E, R, LANE), I32), jax.ShapeDtypeStruct((E, R, LANE), I32),
                   jax.ShapeDtypeStruct((E, cap), I32), jax.ShapeDtypeStruct((E, cap, LANE), F32)],
        scratch_shapes=[pltpu.VMEM((E, R, LANE), F32)],
        compiler_params=pltpu.CompilerParams(vmem_limit_bytes=VMEM_LIMIT),
        name="route",
    )(a3)
    return slot, rowoff[:, :, 0], idx, gate


def _ffn_kernel(idx_ref, x_ref, ng_ref, gate_ref, wg_ref, wu_ref, wd_ref, o_ref, xbuf_ref, hn_ref, acc_ref,
                sem_ref, *, tm):
    n_m = pl.num_programs(1)
    n_f = pl.num_programs(2)
    f = pl.program_id(2)
    tile = pl.program_id(0) * n_m + pl.program_id(1)
    n_tiles = pl.num_programs(0) * n_m
    per_step = tm // n_f

    def row_copy(t, i, buf):
        src = idx_ref[t * tm + i]
        return pltpu.make_async_copy(x_ref.at[pl.ds(src, 1), :], xbuf_ref.at[buf, pl.ds(i, 1), :], sem_ref.at[buf])

    @pl.when((tile == 0) & (f == 0))
    def _():
        lax.fori_loop(0, tm, lambda i, c: (row_copy(0, i, 0).start(), c)[1], 0)

    def wait_rows(buf):
        pltpu.make_async_copy(x_ref.at[pl.ds(0, tm), :], xbuf_ref.at[buf], sem_ref.at[buf]).wait()

    @pl.when(f == 0)
    def _():
        wait_rows(tile % 2)
        hn_ref[...] = _rms(xbuf_ref[tile % 2], ng_ref[...]).astype(BF16)
        acc_ref[...] = jnp.zeros_like(acc_ref)

    nxt = jnp.minimum(tile + 1, n_tiles - 1)
    for i in range(per_step):
        row_copy(nxt, f * per_step + i, (tile + 1) % 2).start()

    x = hn_ref[...]
    g = _dot(x, wg_ref[0, 0].astype(BF16))
    up = _dot(x, wu_ref[0, 0].astype(BF16))
    hid = (g * jax.nn.sigmoid(g) * up).astype(BF16)
    acc_ref[...] += _dot(hid, wd_ref[0, 0].astype(BF16))

    @pl.when(f == pl.num_programs(2) - 1)
    def _():
        gate = gate_ref[0]
        for c in range(D_MODEL // LANE):
            cl = slice(c * LANE, (c + 1) * LANE)
            o_ref[0, :, cl] = (acc_ref[:, cl] * gate).astype(o_ref.dtype)

        @pl.when(tile == n_tiles - 1)
        def _():
            wait_rows((tile + 1) % 2)


def _expert_ffn(x, idx, ng, gate, wg, wu, wd, layer, *, tm, tf):
    D = x.shape[1]
    E, cap, _ = gate.shape
    F = wg.shape[3]
    grid_spec = pltpu.PrefetchScalarGridSpec(
        num_scalar_prefetch=1,
        grid=(E, cap // tm, F // tf),
        in_specs=[pl.BlockSpec(memory_space=pl.ANY),
                  pl.BlockSpec((1, D), lambda e, m, f, *_: (0, 0)),
                  pl.BlockSpec((1, tm, LANE), lambda e, m, f, *_: (e, m, 0)),
                  pl.BlockSpec((1, 1, D, tf), lambda e, m, f, *_: (layer, e, 0, f)),
                  pl.BlockSpec((1, 1, D, tf), lambda e, m, f, *_: (layer, e, 0, f)),
                  pl.BlockSpec((1, 1, tf, D), lambda e, m, f, *_: (layer, e, f, 0))],
        out_specs=pl.BlockSpec((1, tm, D), lambda e, m, f, *_: (e, m, 0)),
        scratch_shapes=[pltpu.VMEM((2, tm, D), F32), pltpu.VMEM((tm, D), BF16), pltpu.VMEM((tm, D), F32),
                        pltpu.SemaphoreType.DMA((2,))],
    )
    return pl.pallas_call(
        functools.partial(_ffn_kernel, tm=tm),
        grid_spec=grid_spec,
        out_shape=jax.ShapeDtypeStruct((E, cap, D), BF16),
        compiler_params=_cparams(("arbitrary", "arbitrary", "arbitrary")),
        name="ffn",
    )(idx, x, ng, gate, wg, wu, wd)


def _combine_plan(slot, rowoff, *, cap):
    E, R = rowoff.shape
    CH = BF16_ROWS
    nxt = jnp.concatenate([rowoff[:, 1:], jnp.full((E, 1), cap, I32)], axis=1)
    cnt = nxt - rowoff
    c0 = rowoff // CH
    nch = jnp.where(cnt > 0, (nxt + CH - 1) // CH - c0, 0)
    cum = jnp.cumsum(nch, axis=0)
    total = cum[-1]
    kc = E * COMBINE_MAX_CHUNKS
    p = jnp.arange(kc, dtype=I32)
    e_of = jnp.minimum(jnp.sum((cum[:, :, None] <= p[None, None, :]).astype(I32), axis=0), E - 1)
    e_is = e_of[None, :, :] == jnp.arange(E, dtype=I32)[:, None, None]
    pick = lambda a: jnp.sum(jnp.where(e_is, a[:, :, None], 0), axis=0)
    in_expert = pick(c0) + (p[None, :] - pick(cum - nch))
    valid = p[None, :] < total[:, None]
    chunk = jnp.where(valid, e_of * (cap // CH) + in_expert, 0)
    row_expert = jnp.broadcast_to(jnp.where(valid, e_of, -1)[:, :, None], (R, kc, CH))
    row_slot = jnp.where(valid[:, :, None], in_expert[:, :, None] * CH + jnp.arange(CH, dtype=I32), -2).astype(F32)
    return (chunk.reshape(-1), total, row_expert.reshape(R, 1, kc * CH), row_slot.reshape(R, 1, kc * CH),
            slot.reshape(E, -1).T)


def _combine_kernel(chunks_ref, total_ref, x_ref, slot_ref, rowe_ref, rows_ref, ye_ref, fg_ref, o_ref,
                    buf_ref, sem_ref, *, kc, final_norm):
    r = pl.program_id(0)
    n_r = pl.num_programs(0)
    CH = BF16_ROWS
    KT = COMBINE_KT
    KTC = KT // CH

    def start_all(rr, b):
        def issue(i, c):
            pltpu.make_async_copy(ye_ref.at[chunks_ref[rr * kc + i]], buf_ref.at[b, i], sem_ref.at[b]).start()
            return c

        lax.fori_loop(0, total_ref[rr], issue, 0)

    @pl.when(r == 0)
    def _():
        buf_ref[...] = jnp.zeros_like(buf_ref)
        start_all(0, 0)

    @pl.when(r + 1 < n_r)
    def _():
        start_all(r + 1, (r + 1) % 2)

    b = r % 2
    n_chunks = total_ref[r]
    for bit in range(kc.bit_length()):
        @pl.when(((n_chunks >> bit) & 1) == 1)
        def _(m=1 << bit):
            pltpu.make_async_copy(ye_ref.at[pl.ds(0, m)], buf_ref.at[b, pl.ds(0, m)], sem_ref.at[b]).wait()

    SPLIT = 64
    slot = slot_ref[...]
    slot_hi = (slot >> 6).astype(F32).astype(BF16)
    slot_lo = (slot & (SPLIT - 1)).astype(F32).astype(BF16)
    expert_iota = lax.broadcasted_iota(I32, (N_EXPERTS, KT), 0)
    o_ref[...] = x_ref[...]

    def k_body(kt, carry):
        k0 = pl.multiple_of(kt * KT, KT)
        of_expert = (expert_iota == rowe_ref[0, :, pl.ds(k0, KT)]).astype(F32).astype(BF16)
        wanted = _dot(slot_hi, of_expert) * SPLIT + _dot(slot_lo, of_expert)
        onehot = jnp.where(wanted == rows_ref[0, :, pl.ds(k0, KT)], 1.0, 0.0).astype(BF16)
        rows = buf_ref[b, pl.ds(pl.multiple_of(kt * KTC, KTC), KTC)].reshape(KT, D_MODEL)
        o_ref[...] += _dot(onehot, rows)
        return carry

    lax.fori_loop(0, (total_ref[r] * CH + KT - 1) // KT, k_body, 0)
    if final_norm:
        o_ref[...] = _rms(o_ref[...], fg_ref[...])


def _combine(x, ye, plan, fg, *, final_norm):
    n, D = x.shape
    E, cap, _ = ye.shape
    R = n // LANE
    chunks, total, row_expert, row_slot, slot_tok = plan
    kc = E * COMBINE_MAX_CHUNKS
    ye_chunks = ye.reshape(E * cap // BF16_ROWS, BF16_ROWS, D)
    row_spec = pl.BlockSpec((1, 1, kc * BF16_ROWS), lambda r, *_: (r, 0, 0))
    grid_spec = pltpu.PrefetchScalarGridSpec(
        num_scalar_prefetch=2,
        grid=(R,),
        in_specs=[pl.BlockSpec((LANE, D), lambda r, *_: (r, 0)),
                  pl.BlockSpec((LANE, E), lambda r, *_: (r, 0)),
                  row_spec, row_spec,
                  pl.BlockSpec(memory_space=pl.ANY),
                  pl.BlockSpec((1, D), lambda r, *_: (0, 0))],
        out_specs=pl.BlockSpec((LANE, D), lambda r, *_: (r, 0)),
        scratch_shapes=[pltpu.VMEM((2, kc, BF16_ROWS, D), BF16), pltpu.SemaphoreType.DMA((2,))],
    )
    return pl.pallas_call(
        functools.partial(_combine_kernel, kc=kc, final_norm=final_norm),
        grid_spec=grid_spec,
        out_shape=jax.ShapeDtypeStruct((n, D), F32),
        compiler_params=_cparams(("arbitrary",)),
        name="combine",
    )(chunks, total, x, slot_tok, row_expert, row_slot, ye_chunks, fg)


def _run_trunk(x, mem, p):
    B, S, D = x.shape
    M = mem.shape[1]
    n = B * S
    cap = CAPACITY_FACTOR * n // N_EXPERTS
    x = x.reshape(n, D)
    mem2 = mem.reshape(B * M, D)
    depth = p["w_main"].shape[0]
    conv_t = min(512, S)
    xa_t = min(1024, S)
    for l in range(depth):
        u, lr = _norm_matmul(x, p["norm_mix_g"][l], p["w_main"][l], p["w_lr"][l],
                             tm=min(1024, n), tn=U_COLS // 4, name="inproj")
        ha = _conv_branch(u, p["dw_w8"][l], p["conv_dw_b"][l], p["conv_ln_g"][l], p["conv_ln_b"][l],
                          B=B, S=S, T=conv_t)
        o_f, o_b = _gla_branch(u, lr, p["w2p"][l], p["gla_gate_b"][l], B=B, S=S)
        kv = _norm_matmul(mem2, p["mem_norm_g"][l], p["w_mem_kv"][l], tm=M, tn=1024, name="memkv")
        hc = _xattn(u, kv, B=B, S=S, M=M, T=xa_t)
        x, aff_t = _merge(ha, o_f, o_b, u, hc, x, p["conv_pw_w"][l], p["gla_o"][l], p["xa_o"][l],
                          p["w_out"][l], p["b_merge"][l], p["gla_norm_g"][l], p["norm_ffn_g"][l],
                          p["w_router_t"][l], T=min(512, n))
        slot, rowoff, idx, gate = _route(aff_t, n=n)
        ye = _expert_ffn(x, idx.reshape(-1), p["norm_ffn_g"][l], gate, p["w_gate_e"], p["w_up_e"],
                         p["w_down_e"], l, tm=min(1024, cap), tf=512)
        x = _combine(x, ye, _combine_plan(slot, rowoff, cap=cap), p["final_norm_g"],
                     final_norm=(l == depth - 1))
    return x.reshape(B, S, D)


def _prepare(norm_mix_g, w_in, b_merge, conv_dw_w, conv_dw_b, conv_ln_g, conv_ln_b, conv_pw_w,
             gla_gate_w2, gla_gate_b, gla_norm_g, gla_o, mem_norm_g, w_mem_kv, xa_o, w_out, norm_ffn_g,
             w_router, w_gate_e, w_up_e, w_down_e, final_norm_g):
    L = w_in.shape[0]
    lr0 = U_XQ
    lr1 = lr0 + 2 * GATE_RANK
    w_main = jnp.concatenate([w_in[:, :, :lr0], w_in[:, :, lr1:]], axis=-1).astype(BF16)
    w_lr = jnp.pad(w_in[:, :, lr0:lr1], ((0, 0), (0, 0), (0, LR_COLS - 2 * GATE_RANK))).astype(BF16)
    w2p = jnp.zeros((L, 2, LR_COLS, D_K), F32)
    w2p = w2p.at[:, 0, 0:GATE_RANK].set(gla_gate_w2[:, 0])
    w2p = w2p.at[:, 1, GATE_RANK:2 * GATE_RANK].set(gla_gate_w2[:, 1])
    row = lambda a: a[:, None, :]
    return dict(
        norm_mix_g=row(norm_mix_g), w_main=w_main, w_lr=w_lr, b_merge=row(b_merge),
        dw_w8=jnp.repeat(conv_dw_w, SUBLANE, axis=1), conv_dw_b=row(conv_dw_b),
        conv_ln_g=row(conv_ln_g), conv_ln_b=row(conv_ln_b), conv_pw_w=conv_pw_w.astype(BF16),
        w2p=w2p, gla_gate_b=gla_gate_b[:, :, None, :], gla_norm_g=row(gla_norm_g), gla_o=gla_o.astype(BF16),
        mem_norm_g=row(mem_norm_g), w_mem_kv=w_mem_kv.astype(BF16), xa_o=xa_o.astype(BF16),
        w_out=w_out.astype(BF16), norm_ffn_g=row(norm_ffn_g), w_router_t=jnp.swapaxes(w_router, 1, 2),
        w_gate_e=w_gate_e, w_up_e=w_up_e, w_down_e=w_down_e,
        final_norm_g=final_norm_g[None, :],
    )


def kernel(x_prompt, x_sample, mem_prompt, mem_sample, norm_mix_g, w_in, b_merge, conv_dw_w, conv_dw_b, conv_ln_g, conv_ln_b, conv_pw_w, gla_gate_w2, gla_gate_b, gla_norm_g, gla_o, mem_norm_g, w_mem_kv, xa_o, w_out, norm_ffn_g, w_router, w_gate_e, w_up_e, w_down_e, final_norm_g):
    p = _prepare(norm_mix_g, w_in, b_merge, conv_dw_w, conv_dw_b, conv_ln_g, conv_ln_b, conv_pw_w,
                 gla_gate_w2, gla_gate_b, gla_norm_g, gla_o, mem_norm_g, w_mem_kv, xa_o, w_out, norm_ffn_g,
                 w_router, w_gate_e, w_up_e, w_down_e, final_norm_g)
    return (_run_trunk(x_prompt, mem_prompt, p), _run_trunk(x_sample, mem_sample, p))
```

```python
import functools

import jax
import jax.numpy as jnp
from jax import lax
from jax.experimental import pallas as pl
from jax.experimental.pallas import tpu as pltpu

F32 = jnp.float32
BF16 = jnp.bfloat16
I32 = jnp.int32

D_MODEL = 1024
CONV_K = 31
CONV_HALO = 16
GLA_HEADS = 4
D_K = 512
D_V = 1024
DK_HEAD = D_K // GLA_HEADS
DV_HEAD = D_V // GLA_HEADS
GATE_RANK = 16
GATE_TEMP = 16.0
XA_HEADS = 4
XA_HEAD = D_MODEL // XA_HEADS
N_EXPERTS = 16
D_FF = 2048
CAPACITY_FACTOR = 2
EPS = 1e-6

LANE = 128
SUBLANE = 8
BF16_ROWS = 16
VMEM_LIMIT = 56 * 1024 * 1024

U_A, U_B, U_Q, U_K, U_V, U_G, U_XQ, U_MG = 0, 1024, 2048, 2560, 3072, 4096, 5120, 6144
U_COLS = 9216
LR_COLS = LANE

GLA_CHUNK = 128
GLA_CHUNKS_PER_STEP = 4
COMBINE_MAX_CHUNKS = LANE // BF16_ROWS + 1
COMBINE_KT = 256


def _cparams(sem):
    return pltpu.CompilerParams(dimension_semantics=sem, vmem_limit_bytes=VMEM_LIMIT)


def _split_bf16(a):
    hi = a.astype(BF16)
    lo = (a - hi.astype(F32)).astype(BF16)
    return hi, lo


def _dot(a, b, dims=None):
    if dims is None:
        return jnp.dot(a, b, preferred_element_type=F32)
    return lax.dot_general(a, b, (dims, ((), ())), preferred_element_type=F32)


def _dot3(a, b, dims=None):
    ah, al = _split_bf16(a)
    bh, bl = _split_bf16(b)
    return _dot(ah, bh, dims) + (_dot(al, bh, dims) + _dot(ah, bl, dims))


def _rms(x, g):
    return x * lax.rsqrt(jnp.mean(x * x, axis=-1, keepdims=True) + EPS) * g


def _normmm_kernel(x_ref, g_ref, w_ref, *rest, with_lr):
    if with_lr:
        wlr_ref, u_ref, lr_ref, hn_ref = rest
    else:
        u_ref, hn_ref = rest

    @pl.when(pl.program_id(1) == 0)
    def _():
        hn = _rms(x_ref[...], g_ref[...]).astype(BF16)
        hn_ref[...] = hn
        if with_lr:
            lr_ref[...] = _dot(hn, wlr_ref[...])

    u_ref[...] = _dot(hn_ref[...], w_ref[...]).astype(u_ref.dtype)


def _norm_matmul(x, g, w, w_lr=None, *, tm, tn, name):
    n, d = x.shape
    cols = w.shape[1]
    with_lr = w_lr is not None
    in_specs = [pl.BlockSpec((tm, d), lambda i, j: (i, 0)),
                pl.BlockSpec((1, d), lambda i, j: (0, 0)),
                pl.BlockSpec((d, tn), lambda i, j: (0, j))]
    out_shape = [jax.ShapeDtypeStruct((n, cols), BF16)]
    out_specs = [pl.BlockSpec((tm, tn), lambda i, j: (i, j))]
    args = [x, g, w]
    if with_lr:
        in_specs.append(pl.BlockSpec((d, LR_COLS), lambda i, j: (0, 0)))
        out_shape.append(jax.ShapeDtypeStruct((n, LR_COLS), F32))
        out_specs.append(pl.BlockSpec((tm, LR_COLS), lambda i, j: (i, 0)))
        args.append(w_lr)
    out = pl.pallas_call(
        functools.partial(_normmm_kernel, with_lr=with_lr),
        grid=(n // tm, cols // tn),
        in_specs=in_specs, out_specs=out_specs, out_shape=out_shape,
        scratch_shapes=[pltpu.VMEM((tm, d), BF16)],
        compiler_params=_cparams(("parallel", "arbitrary")),
        name=name,
    )(*args)
    return out if with_lr else out[0]


def _conv_kernel(ac_ref, bc_ref, ap_ref, bp_ref, an_ref, bn_ref, w_ref, cb_ref, g_ref, be_ref,
                 o_ref, hs_ref, co_ref, *, T):
    i = pl.program_id(1)
    n_t = pl.num_programs(1)
    H = CONV_HALO
    rows = T + 2 * H

    def glu(a_ref, b_ref):
        return a_ref[...].astype(F32) * jax.nn.sigmoid(b_ref[...].astype(F32))

    hs_ref[0, 0:H, :] = jnp.where(i > 0, glu(ap_ref, bp_ref), 0.0)
    hs_ref[0, H:H + T, :] = glu(ac_ref, bc_ref)
    hs_ref[0, H + T:rows, :] = jnp.where(i < n_t - 1, glu(an_ref, bn_ref), 0.0)
    for s in range(1, SUBLANE):
        hs_ref[s, 0:rows - SUBLANE, :] = hs_ref[0, s:s + rows - SUBLANE, :]

    RG = 64
    first = H - CONV_K // 2
    for c in range(D_MODEL // LANE):
        cl = slice(c * LANE, (c + 1) * LANE)
        taps = [w_ref[k * SUBLANE:(k + 1) * SUBLANE, cl] for k in range(CONV_K)]
        bias = jnp.broadcast_to(cb_ref[:, cl], (SUBLANE, LANE))

        def body(g, carry, cl=cl, taps=taps, bias=bias):
            r0 = pl.multiple_of(g * RG, RG)
            for v in range(RG // SUBLANE):
                acc = bias
                for k in range(CONV_K):
                    a, s = divmod(first + k, SUBLANE)
                    acc = acc + hs_ref[s, pl.ds(r0 + (a + v) * SUBLANE, SUBLANE), cl] * taps[k]
                co_ref[pl.ds(r0 + v * SUBLANE, SUBLANE), cl] = acc
            return carry

        lax.fori_loop(0, T // RG, body, 0)

    LG = 64
    for r0 in range(0, T, LG):
        hf = co_ref[r0:r0 + LG, :]
        mu = jnp.mean(hf, axis=-1, keepdims=True)
        cen = hf - mu
        var = jnp.mean(cen * cen, axis=-1, keepdims=True)
        y = cen * lax.rsqrt(var + EPS) * g_ref[...] + be_ref[...]
        o_ref[r0:r0 + LG, :] = (y * jax.nn.sigmoid(y)).astype(o_ref.dtype)


def _conv_branch(u, dw_w8, dw_b, ln_g, ln_b, *, B, S, T):
    n = B * S
    n_t = S // T
    hb = T // CONV_HALO
    last_hb = n // CONV_HALO - 1
    D = D_MODEL

    def cur(col):
        return pl.BlockSpec((T, D), lambda b, i: (b * n_t + i, col))

    def prev(col):
        return pl.BlockSpec((CONV_HALO, D), lambda b, i: (jnp.maximum((b * n_t + i) * hb - 1, 0), col))

    def nxt(col):
        return pl.BlockSpec((CONV_HALO, D), lambda b, i: (jnp.minimum((b * n_t + i + 1) * hb, last_hb), col))

    def const(shape):
        return pl.BlockSpec(shape, lambda b, i: (0, 0))

    ca, cb = U_A // D, U_B // D
    return pl.pallas_call(
        functools.partial(_conv_kernel, T=T),
        grid=(B, n_t),
        in_specs=[cur(ca), cur(cb), prev(ca), prev(cb), nxt(ca), nxt(cb),
                  const((CONV_K * SUBLANE, D)), const((1, D)), const((1, D)), const((1, D))],
        out_specs=pl.BlockSpec((T, D), lambda b, i: (b * n_t + i, 0)),
        out_shape=jax.ShapeDtypeStruct((n, D), BF16),
        scratch_shapes=[pltpu.VMEM((SUBLANE, T + 2 * CONV_HALO, D), F32), pltpu.VMEM((T, D), F32)],
        compiler_params=_cparams(("parallel", "parallel")),
        name="conv",
    )(u, u, u, u, u, u, dw_w8, dw_b, ln_g, ln_b)


def _gla_kernel(qf_ref, kf_ref, vf_ref, lf_ref, qb_ref, kb_ref, vb_ref, lb_ref, w2_ref, bg_ref,
                of_ref, ob_ref, st_ref, *, C, G):
    @pl.when(pl.program_id(1) == 0)
    def _():
        st_ref[...] = jnp.zeros_like(st_ref)

    row = lax.broadcasted_iota(I32, (C, C), 0)
    col = lax.broadcasted_iota(I32, (C, C), 1)
    scale = DK_HEAD ** -0.5

    def direction(d, q_ref, k_ref, v_ref, l_ref, o_ref):
        fwd = d == 0
        z = _dot3(l_ref[...], w2_ref[d]) + bg_ref[d]
        la = (jnp.minimum(z, 0.0) - jnp.log1p(jnp.exp(-jnp.abs(z)))) * (1.0 / GATE_TEMP)
        la_hi, la_lo = _split_bf16(la)
        tri = ((row >= col) if fwd else (row <= col)).astype(BF16)
        mask = (row >= col) if fwd else (row < col)
        states = [st_ref[d, h] for h in range(GLA_HEADS)]
        for g in (range(G) if fwd else range(G - 1, -1, -1)):
            rows = slice(g * C, (g + 1) * C)
            b = _dot(tri, la_hi[rows]) + _dot(tri, la_lo[rows])
            b_far = b[C - 1:C, :] if fwd else b[0:1, :]
            q = q_ref[rows, :].astype(F32) * scale
            k = k_ref[rows, :].astype(F32)
            qs = (q * jnp.exp(b)).astype(BF16)
            ks = (k * jnp.exp(-b)).astype(BF16)
            kd = (k * jnp.exp(b_far - b)).astype(BF16)
            carry_decay = jnp.exp(b_far)
            for h in range(GLA_HEADS):
                kl = slice(h * DK_HEAD, (h + 1) * DK_HEAD)
                vl = slice(h * DV_HEAD, (h + 1) * DV_HEAD)
                attn = _dot(qs[:, kl], ks[:, kl], ((1,), (1,)))
                attn = jnp.where(mask, attn, 0.0).astype(BF16)
                v_h = v_ref[rows, vl]
                o_ref[rows, vl] = _dot(attn, v_h) + _dot(qs[:, kl], states[h].astype(BF16), ((1,), (1,)))
                states[h] = states[h] * carry_decay[:, kl] + _dot(v_h, kd[:, kl], ((0,), (0,)))
        for h in range(GLA_HEADS):
            st_ref[d, h] = states[h]

    direction(0, qf_ref, kf_ref, vf_ref, lf_ref, of_ref)
    direction(1, qb_ref, kb_ref, vb_ref, lb_ref, ob_ref)


def _gla_branch(u, lr, w2p, bg, *, B, S):
    n = B * S
    G = GLA_CHUNKS_PER_STEP
    C = GLA_CHUNK * G
    n_c = S // C

    def fwd_map(col):
        return lambda b, c: (b * n_c + c, col)

    def bwd_map(col):
        return lambda b, c: (b * n_c + (n_c - 1 - c), col)

    def specs(m):
        return [pl.BlockSpec((C, D_K), m(U_Q // D_K)), pl.BlockSpec((C, D_K), m(U_K // D_K)),
                pl.BlockSpec((C, D_V), m(U_V // D_V)), pl.BlockSpec((C, LR_COLS), m(0))]

    return pl.pallas_call(
        functools.partial(_gla_kernel, C=GLA_CHUNK, G=G),
        grid=(B, n_c),
        in_specs=specs(fwd_map) + specs(bwd_map) + [
            pl.BlockSpec((2, LR_COLS, D_K), lambda b, c: (0, 0, 0)),
            pl.BlockSpec((2, 1, D_K), lambda b, c: (0, 0, 0))],
        out_specs=[pl.BlockSpec((C, D_V), fwd_map(0)), pl.BlockSpec((C, D_V), bwd_map(0))],
        out_shape=[jax.ShapeDtypeStruct((n, D_V), F32), jax.ShapeDtypeStruct((n, D_V), F32)],
        scratch_shapes=[pltpu.VMEM((2, GLA_HEADS, DV_HEAD, DK_HEAD), F32)],
        compiler_params=_cparams(("parallel", "arbitrary")),
        name="gla",
    )(u, u, u, lr, u, u, u, lr, w2p, bg)


def _xattn_kernel(q_ref, k_ref, v_ref, o_ref):
    scale = XA_HEAD ** -0.5
    for h in range(XA_HEADS):
        hl = slice(h * XA_HEAD, (h + 1) * XA_HEAD)
        s = _dot(q_ref[:, hl], k_ref[:, hl], ((1,), (1,))) * scale
        p = jnp.exp(s - jnp.max(s, axis=-1, keepdims=True))
        o = _dot(p.astype(BF16), v_ref[:, hl]) / jnp.sum(p, axis=-1, keepdims=True)
        o_ref[:, hl] = o.astype(o_ref.dtype)


def _xattn(u, kv, *, B, S, M, T):
    n_t = S // T
    D = D_MODEL
    return pl.pallas_call(
        _xattn_kernel,
        grid=(B, n_t),
        in_specs=[pl.BlockSpec((T, D), lambda b, i: (b * n_t + i, U_XQ // D)),
                  pl.BlockSpec((M, D), lambda b, i: (b, 0)),
                  pl.BlockSpec((M, D), lambda b, i: (b, 1))],
        out_specs=pl.BlockSpec((T, D), lambda b, i: (b * n_t + i, 0)),
        out_shape=jax.ShapeDtypeStruct((B * S, D), BF16),
        compiler_params=_cparams(("parallel", "parallel")),
        name="xattn",
    )(u, kv, kv)


def _merge_kernel(ha_ref, of_ref, ob_ref, go_ref, hc_ref, mga_ref, mgb_ref, mgc_ref, x_ref,
                  pw_ref, glao_ref, xao_ref, wout_ref, bm_ref, gng_ref, fng_ref, wr_ref,
                  xo_ref, aff_ref):
    o = of_ref[...] + ob_ref[...]
    parts = []
    for h in range(GLA_HEADS):
        oh = o[:, h * DV_HEAD:(h + 1) * DV_HEAD]
        parts.append(oh * lax.rsqrt(jnp.mean(oh * oh, axis=-1, keepdims=True) + EPS))
    on = jnp.concatenate(parts, axis=-1) * gng_ref[...]
    go = go_ref[...].astype(F32)
    hb = (on * (go * jax.nn.sigmoid(go))).astype(BF16)

    bm = bm_ref[...]
    D = D_MODEL

    def gate(mg_ref, j):
        return jax.nn.sigmoid(mg_ref[...].astype(F32) + bm[:, j * D:(j + 1) * D])

    merged = gate(mga_ref, 0) * _dot(ha_ref[...], pw_ref[...])
    merged = merged + gate(mgb_ref, 1) * _dot(hb, glao_ref[...])
    merged = merged + gate(mgc_ref, 2) * _dot(hc_ref[...], xao_ref[...])
    x_new = x_ref[...] + _dot(merged.astype(BF16), wout_ref[...])
    xo_ref[...] = x_new

    h2 = _rms(x_new, fng_ref[...])
    logits = _dot3(wr_ref[...], h2, ((1,), (1,)))
    e = jnp.exp(logits - jnp.max(logits, axis=0, keepdims=True))
    aff_ref[...] = e / jnp.sum(e, axis=0, keepdims=True)


def _merge(ha, o_f, o_b, u, hc, x, pw, glao, xao, wout, bm, gng, fng, wr_t, *, T):
    n, D = x.shape

    def tile(col=0):
        return pl.BlockSpec((T, D), lambda i: (i, col))

    def const(shape):
        return pl.BlockSpec(shape, lambda i: (0, 0), pipeline_mode=pl.Buffered(1))

    mg0 = U_MG // D
    return pl.pallas_call(
        _merge_kernel,
        grid=(n // T,),
        in_specs=[tile(), tile(), tile(), tile(U_G // D), tile(), tile(mg0), tile(mg0 + 1), tile(mg0 + 2),
                  tile(), const((D, D)), const((D, D)), const((D, D)), const((D, D)),
                  const((1, 3 * D)), const((1, D)), const((1, D)), const((N_EXPERTS, D))],
        out_specs=[tile(), pl.BlockSpec((N_EXPERTS, T), lambda i: (0, i))],
        out_shape=[jax.ShapeDtypeStruct((n, D), F32), jax.ShapeDtypeStruct((N_EXPERTS, n), F32)],
        compiler_params=_cparams(("parallel",)),
        name="merge",
    )(ha, o_f, o_b, u, hc, u, u, u, x, pw, glao, xao, wout, bm, gng, fng, wr_t)


def _route_kernel(a_ref, slot_ref, rowoff_ref, idx_ref, gate_ref, incl_ref, *, R, cap):
    E = N_EXPERTS
    bits = pltpu.bitcast(a_ref[...], I32)

    thr = jnp.zeros((E, 1, 1), I32)
    for bit in range(30, -1, -1):
        cand = thr | (1 << bit)
        cnt = jnp.sum((bits >= cand).astype(I32), axis=(1, 2), keepdims=True)
        thr = jnp.where(cnt >= cap, cand, thr)
    gt = bits > thr
    eq = bits == thr
    need = (cap - jnp.sum(gt.astype(I32), axis=(1, 2), keepdims=True)).astype(F32)

    li0 = lax.broadcasted_iota(I32, (LANE, LANE), 0)
    li1 = lax.broadcasted_iota(I32, (LANE, LANE), 1)
    upper = (li0 <= li1).astype(BF16)
    ri0 = lax.broadcasted_iota(I32, (R, R), 0)
    ri1 = lax.broadcasted_iota(I32, (R, R), 1)
    lower_strict = (ri0 > ri1).astype(BF16)
    upper_r = (ri0 <= ri1).astype(BF16)

    def prefix(m):
        incl = _dot(m.astype(BF16).reshape(E * R, LANE), upper).reshape(E, R, LANE)
        tot = jnp.broadcast_to(incl[:, :, LANE - 1:LANE], (E, R, LANE)).astype(BF16)
        off = jnp.stack([_dot(lower_strict, tot[e]) for e in range(E)])
        return incl, off

    gtf, eqf = gt.astype(F32), eq.astype(F32)
    incl_gt, off_gt = prefix(gtf)
    incl_eq, off_eq = prefix(eqf)
    excl_eq = incl_eq - eqf + off_eq
    sel = gt | (eq & (excl_eq < need))
    self_ = sel.astype(F32)
    rowoff = off_gt + jnp.minimum(off_eq, need)
    pos = (incl_gt - gtf + off_gt) + jnp.minimum(excl_eq, need)
    slot_ref[...] = jnp.where(sel, pos, -1.0).astype(I32)
    rowoff_ref[...] = rowoff.astype(I32)
    incl_ref[...] = pos - rowoff + self_

    s_col = lax.broadcasted_iota(I32, (cap, R), 0).astype(F32)
    s_col_l = lax.broadcasted_iota(I32, (cap, LANE), 0).astype(F32)
    ones8 = jnp.ones((SUBLANE, LANE), BF16)
    r_lane = lax.broadcasted_iota(I32, (SUBLANE, R), 1).astype(BF16)

    def idx_body(e, carry):
        incl_e = incl_ref[e]
        sel_e = (slot_ref[e] >= 0).astype(BF16)
        off_e = rowoff_ref[e].astype(F32)
        tot_lane = _dot(ones8, sel_e, ((1,), (1,)))
        cum_lane = _dot(tot_lane.astype(BF16), upper_r)
        cum1 = cum_lane[0:1, :]
        start1 = cum1 - tot_lane[0:1, :]
        onehot = ((start1 <= s_col) & (s_col < cum1)).astype(BF16)
        off_hi = jnp.floor(off_e * (1.0 / LANE))
        off_lo = off_e - off_hi * LANE
        off_s = _dot(onehot, off_hi.astype(BF16)) * LANE + _dot(onehot, off_lo.astype(BF16))
        incl_row = _dot(onehot, incl_e.astype(BF16))
        rank = s_col_l - off_s
        before = (incl_row <= rank).astype(BF16)
        col_lane = _dot(ones8, before, ((1,), (1,)))
        row_lane = _dot(r_lane, onehot, ((1,), (1,)))
        idx_ref[pl.ds(e, 1), :] = (row_lane[0:1, :] * LANE + col_lane[0:1, :]).astype(I32)
        a_e = a_ref[e]
        a_hi = a_e.astype(BF16)
        a_mid = (a_e - a_hi.astype(F32)).astype(BF16)
        a_lo = (a_e - a_hi.astype(F32) - a_mid.astype(F32)).astype(BF16)
        aff_row = _dot(onehot, a_hi) + (_dot(onehot, a_mid) + _dot(onehot, a_lo))
        at_col = (incl_row == rank + 1.0) & (_dot(onehot, sel_e) > 0.5)
        gate = jnp.sum(jnp.where(at_col, aff_row, 0.0), axis=-1, keepdims=True)
        gate_ref[e] = jnp.broadcast_to(gate, (cap, LANE))
        return carry

    lax.fori_loop(0, E, idx_body, 0)


def _route(aff_t, *, n):
    R = n // LANE
    cap = CAPACITY_FACTOR * n // N_EXPERTS
    E = N_EXPERTS
    a3 = aff_t.reshape(E, R, LANE)
    slot, rowoff, idx, gate = pl.pallas_call(
        functools.partial(_route_kernel, R=R, cap=cap),
        out_shape=[jax.ShapeDtypeStruct((E, R, LANE), I32), jax.ShapeDtypeStruct((E, R, LANE), I32),
                   jax.ShapeDtypeStruct((E, cap), I32), jax.ShapeDtypeStruct((E, cap, LANE), F32)],
        scratch_shapes=[pltpu.VMEM((E, R, LANE), F32)],
        compiler_params=pltpu.CompilerParams(vmem_limit_bytes=VMEM_LIMIT),
        name="route",
    )(a3)
    return slot, rowoff[:, :, 0], idx, gate


def _ffn_kernel(idx_ref, x_ref, ng_ref, gate_ref, wg_ref, wu_ref, wd_ref, o_ref, xbuf_ref, hn_ref, acc_ref,
                sem_ref, *, tm):
    n_m = pl.num_programs(1)
    n_f = pl.num_programs(2)
    f = pl.program_id(2)
    tile = pl.program_id(0) * n_m + pl.program_id(1)
    n_tiles = pl.num_programs(0) * n_m
    per_step = tm // n_f

    def row_copy(t, i, buf):
        src = idx_ref[t * tm + i]
        return pltpu.make_async_copy(x_ref.at[src], xbuf_ref.at[buf, pl.ds(pl.multiple_of(i * SUBLANE, SUBLANE), SUBLANE), :],
                                     sem_ref.at[buf])

    @pl.when((tile == 0) & (f == 0))
    def _():
        lax.fori_loop(0, tm, lambda i, c: (row_copy(0, i, 0).start(), c)[1], 0)

    def wait_rows(buf):
        pltpu.make_async_copy(x_ref.at[pl.ds(0, tm)], xbuf_ref.at[buf].reshape(tm, SUBLANE, LANE), sem_ref.at[buf]).wait()

    @pl.when(f == 0)
    def _():
        wait_rows(tile % 2)
        parts = [xbuf_ref[tile % 2, pl.ds(j, tm, stride=SUBLANE), :] for j in range(SUBLANE)]
        ms = sum(jnp.sum(p * p, axis=-1, keepdims=True) for p in parts) * (1.0 / D_MODEL)
        inv = lax.rsqrt(ms + EPS)
        for j in range(SUBLANE):
            cl = slice(j * LANE, (j + 1) * LANE)
            hn_ref[:, cl] = (parts[j] * inv * ng_ref[:, cl]).astype(BF16)
        acc_ref[...] = jnp.zeros_like(acc_ref)

    nxt = jnp.minimum(tile + 1, n_tiles - 1)
    for i in range(per_step):
        row_copy(nxt, f * per_step + i, (tile + 1) % 2).start()

    x = hn_ref[...]
    g = _dot(x, wg_ref[0, 0].astype(BF16))
    up = _dot(x, wu_ref[0, 0].astype(BF16))
    hid = (g * jax.nn.sigmoid(g) * up).astype(BF16)
    acc_ref[...] += _dot(hid, wd_ref[0, 0].astype(BF16))

    @pl.when(f == pl.num_programs(2) - 1)
    def _():
        gate = gate_ref[0]
        for c in range(D_MODEL // LANE):
            cl = slice(c * LANE, (c + 1) * LANE)
            o_ref[0, :, cl] = (acc_ref[:, cl] * gate).astype(o_ref.dtype)

        @pl.when(tile == n_tiles - 1)
        def _():
            wait_rows((tile + 1) % 2)


def _expert_ffn(x, idx, ng, gate, wg, wu, wd, layer, *, tm, tf):
    D = D_MODEL
    E, cap, _ = gate.shape
    F = wg.shape[3]
    grid_spec = pltpu.PrefetchScalarGridSpec(
        num_scalar_prefetch=1,
        grid=(E, cap // tm, F // tf),
        in_specs=[pl.BlockSpec(memory_space=pl.ANY),
                  pl.BlockSpec((1, D), lambda e, m, f, *_: (0, 0)),
                  pl.BlockSpec((1, tm, LANE), lambda e, m, f, *_: (e, m, 0)),
                  pl.BlockSpec((1, 1, D, tf), lambda e, m, f, *_: (layer, e, 0, f)),
                  pl.BlockSpec((1, 1, D, tf), lambda e, m, f, *_: (layer, e, 0, f)),
                  pl.BlockSpec((1, 1, tf, D), lambda e, m, f, *_: (layer, e, f, 0))],
        out_specs=pl.BlockSpec((1, tm, D), lambda e, m, f, *_: (e, m, 0)),
        scratch_shapes=[pltpu.VMEM((2, tm * SUBLANE, LANE), F32), pltpu.VMEM((tm, D), BF16), pltpu.VMEM((tm, D), F32),
                        pltpu.SemaphoreType.DMA((2,))],
    )
    return pl.pallas_call(
        functools.partial(_ffn_kernel, tm=tm),
        grid_spec=grid_spec,
        out_shape=jax.ShapeDtypeStruct((E, cap, D), BF16),
        compiler_params=_cparams(("arbitrary", "arbitrary", "arbitrary")),
        name="ffn",
    )(idx, x, ng, gate, wg, wu, wd)


def _combine_plan(slot, rowoff, *, cap):
    E, R = rowoff.shape
    CH = BF16_ROWS
    nxt = jnp.concatenate([rowoff[:, 1:], jnp.full((E, 1), cap, I32)], axis=1)
    cnt = nxt - rowoff
    c0 = rowoff // CH
    nch = jnp.where(cnt > 0, (nxt + CH - 1) // CH - c0, 0)
    cum = jnp.cumsum(nch, axis=0)
    total = cum[-1]
    kc = E * COMBINE_MAX_CHUNKS
    p = jnp.arange(kc, dtype=I32)
    e_of = jnp.minimum(jnp.sum((cum[:, :, None] <= p[None, None, :]).astype(I32), axis=0), E - 1)
    e_is = e_of[None, :, :] == jnp.arange(E, dtype=I32)[:, None, None]
    pick = lambda a: jnp.sum(jnp.where(e_is, a[:, :, None], 0), axis=0)
    in_expert = pick(c0) + (p[None, :] - pick(cum - nch))
    valid = p[None, :] < total[:, None]
    chunk = jnp.where(valid, e_of * (cap // CH) + in_expert, 0)
    row_expert = jnp.broadcast_to(jnp.where(valid, e_of, -1)[:, :, None], (R, kc, CH))
    row_slot = jnp.where(valid[:, :, None], in_expert[:, :, None] * CH + jnp.arange(CH, dtype=I32), -2).astype(F32)
    return (chunk.reshape(-1), total, row_expert.reshape(R, 1, kc * CH), row_slot.reshape(R, 1, kc * CH),
            slot.reshape(E, -1).T)


def _combine_kernel(chunks_ref, total_ref, x_ref, slot_ref, rowe_ref, rows_ref, ye_ref, fg_ref, o_ref,
                    buf_ref, sem_ref, *, kc, final_norm):
    r = pl.program_id(0)
    n_r = pl.num_programs(0)
    CH = BF16_ROWS
    KT = COMBINE_KT
    KTC = KT // CH

    def start_all(rr, b):
        def issue(i, c):
            pltpu.make_async_copy(ye_ref.at[chunks_ref[rr * kc + i]], buf_ref.at[b, i], sem_ref.at[b]).start()
            return c

        lax.fori_loop(0, total_ref[rr], issue, 0)

    @pl.when(r == 0)
    def _():
        buf_ref[...] = jnp.zeros_like(buf_ref)
        start_all(0, 0)

    @pl.when(r + 1 < n_r)
    def _():
        start_all(r + 1, (r + 1) % 2)

    b = r % 2
    n_chunks = total_ref[r]
    for bit in range(kc.bit_length()):
        @pl.when(((n_chunks >> bit) & 1) == 1)
        def _(m=1 << bit):
            pltpu.make_async_copy(ye_ref.at[pl.ds(0, m)], buf_ref.at[b, pl.ds(0, m)], sem_ref.at[b]).wait()

    SPLIT = 64
    slot = slot_ref[...]
    slot_hi = (slot >> 6).astype(F32).astype(BF16)
    slot_lo = (slot & (SPLIT - 1)).astype(F32).astype(BF16)
    expert_iota = lax.broadcasted_iota(I32, (N_EXPERTS, KT), 0)
    o_ref[...] = x_ref[...]

    def k_body(kt, carry):
        k0 = pl.multiple_of(kt * KT, KT)
        of_expert = (expert_iota == rowe_ref[0, :, pl.ds(k0, KT)]).astype(F32).astype(BF16)
        wanted = _dot(slot_hi, of_expert) * SPLIT + _dot(slot_lo, of_expert)
        onehot = jnp.where(wanted == rows_ref[0, :, pl.ds(k0, KT)], 1.0, 0.0).astype(BF16)
        rows = buf_ref[b, pl.ds(pl.multiple_of(kt * KTC, KTC), KTC)].reshape(KT, D_MODEL)
        o_ref[...] += _dot(onehot, rows)
        return carry

    lax.fori_loop(0, (total_ref[r] * CH + KT - 1) // KT, k_body, 0)
    if final_norm:
        o_ref[...] = _rms(o_ref[...], fg_ref[...])


def _combine(x, ye, plan, fg, *, final_norm):
    n, D = x.shape
    E, cap, _ = ye.shape
    R = n // LANE
    chunks, total, row_expert, row_slot, slot_tok = plan
    kc = E * COMBINE_MAX_CHUNKS
    ye_chunks = ye.reshape(E * cap // BF16_ROWS, BF16_ROWS, D)
    row_spec = pl.BlockSpec((1, 1, kc * BF16_ROWS), lambda r, *_: (r, 0, 0))
    grid_spec = pltpu.PrefetchScalarGridSpec(
        num_scalar_prefetch=2,
        grid=(R,),
        in_specs=[pl.BlockSpec((LANE, D), lambda r, *_: (r, 0)),
                  pl.BlockSpec((LANE, E), lambda r, *_: (r, 0)),
                  row_spec, row_spec,
                  pl.BlockSpec(memory_space=pl.ANY),
                  pl.BlockSpec((1, D), lambda r, *_: (0, 0))],
        out_specs=pl.BlockSpec((LANE, D), lambda r, *_: (r, 0)),
        scratch_shapes=[pltpu.VMEM((2, kc, BF16_ROWS, D), BF16), pltpu.SemaphoreType.DMA((2,))],
    )
    return pl.pallas_call(
        functools.partial(_combine_kernel, kc=kc, final_norm=final_norm),
        grid_spec=grid_spec,
        out_shape=jax.ShapeDtypeStruct((n, D), F32),
        compiler_params=_cparams(("arbitrary",)),
        name="combine",
    )(chunks, total, x, slot_tok, row_expert, row_slot, ye_chunks, fg)


def _run_trunk(x, mem, p):
    B, S, D = x.shape
    M = mem.shape[1]
    n = B * S
    cap = CAPACITY_FACTOR * n // N_EXPERTS
    x = x.reshape(n, D)
    mem2 = mem.reshape(B * M, D)
    depth = p["w_main"].shape[0]
    conv_t = min(512, S)
    xa_t = min(1024, S)
    for l in range(depth):
        u, lr = _norm_matmul(x, p["norm_mix_g"][l], p["w_main"][l], p["w_lr"][l],
                             tm=min(1024, n), tn=U_COLS // 4, name="inproj")
        ha = _conv_branch(u, p["dw_w8"][l], p["conv_dw_b"][l], p["conv_ln_g"][l], p["conv_ln_b"][l],
                          B=B, S=S, T=conv_t)
        o_f, o_b = _gla_branch(u, lr, p["w2p"][l], p["gla_gate_b"][l], B=B, S=S)
        kv = _norm_matmul(mem2, p["mem_norm_g"][l], p["w_mem_kv"][l], tm=M, tn=1024, name="memkv")
        hc = _xattn(u, kv, B=B, S=S, M=M, T=xa_t)
        x, aff_t = _merge(ha, o_f, o_b, u, hc, x, p["conv_pw_w"][l], p["gla_o"][l], p["xa_o"][l],
                          p["w_out"][l], p["b_merge"][l], p["gla_norm_g"][l], p["norm_ffn_g"][l],
                          p["w_router_t"][l], T=min(512, n))
        slot, rowoff, idx, gate = _route(aff_t, n=n)
        ye = _expert_ffn(x.reshape(n, SUBLANE, LANE), idx.reshape(-1), p["norm_ffn_g"][l], gate, p["w_gate_e"], p["w_up_e"],
                         p["w_down_e"], l, tm=min(1024, cap), tf=512)
        x = _combine(x, ye, _combine_plan(slot, rowoff, cap=cap), p["final_norm_g"],
                     final_norm=(l == depth - 1))
    return x.reshape(B, S, D)


def _prepare(norm_mix_g, w_in, b_merge, conv_dw_w, conv_dw_b, conv_ln_g, conv_ln_b, conv_pw_w,
             gla_gate_w2, gla_gate_b, gla_norm_g, gla_o, mem_norm_g, w_mem_kv, xa_o, w_out, norm_ffn_g,
             w_router, w_gate_e, w_up_e, w_down_e, final_norm_g):
    L = w_in.shape[0]
    lr0 = U_XQ
    lr1 = lr0 + 2 * GATE_RANK
    w_main = jnp.concatenate([w_in[:, :, :lr0], w_in[:, :, lr1:]], axis=-1).astype(BF16)
    w_lr = jnp.pad(w_in[:, :, lr0:lr1], ((0, 0), (0, 0), (0, LR_COLS - 2 * GATE_RANK))).astype(BF16)
    w2p = jnp.zeros((L, 2, LR_COLS, D_K), F32)
    w2p = w2p.at[:, 0, 0:GATE_RANK].set(gla_gate_w2[:, 0])
    w2p = w2p.at[:, 1, GATE_RANK:2 * GATE_RANK].set(gla_gate_w2[:, 1])
    row = lambda a: a[:, None, :]
    return dict(
        norm_mix_g=row(norm_mix_g), w_main=w_main, w_lr=w_lr, b_merge=row(b_merge),
        dw_w8=jnp.repeat(conv_dw_w, SUBLANE, axis=1), conv_dw_b=row(conv_dw_b),
        conv_ln_g=row(conv_ln_g), conv_ln_b=row(conv_ln_b), conv_pw_w=conv_pw_w.astype(BF16),
        w2p=w2p, gla_gate_b=gla_gate_b[:, :, None, :], gla_norm_g=row(gla_norm_g), gla_o=gla_o.astype(BF16),
        mem_norm_g=row(mem_norm_g), w_mem_kv=w_mem_kv.astype(BF16), xa_o=xa_o.astype(BF16),
        w_out=w_out.astype(BF16), norm_ffn_g=row(norm_ffn_g), w_router_t=jnp.swapaxes(w_router, 1, 2),
        w_gate_e=w_gate_e, w_up_e=w_up_e, w_down_e=w_down_e,
        final_norm_g=final_norm_g[None, :],
    )


def kernel(x_prompt, x_sample, mem_prompt, mem_sample, norm_mix_g, w_in, b_merge, conv_dw_w, conv_dw_b, conv_ln_g, conv_ln_b, conv_pw_w, gla_gate_w2, gla_gate_b, gla_norm_g, gla_o, mem_norm_g, w_mem_kv, xa_o, w_out, norm_ffn_g, w_router, w_gate_e, w_up_e, w_down_e, final_norm_g):
    p = _prepare(norm_mix_g, w_in, b_merge, conv_dw_w, conv_dw_b, conv_ln_g, conv_ln_b, conv_pw_w,
                 gla_gate_w2, gla_gate_b, gla_norm_g, gla_o, mem_norm_g, w_mem_kv, xa_o, w_out, norm_ffn_g,
                 w_router, w_gate_e, w_up_e, w_down_e, final_norm_g)
    return (_run_trunk(x_prompt, mem_prompt, p), _run_trunk(x_sample, mem_sample, p))
```
